```python
import math
import jax, jax.numpy as jnp
from jax import lax
import numpy as np

D_MODEL = 4096
BATCH = 1
SEQ = 8192
DEPTH = 2

HEAD_DIM = 128
MIX_WIDTH = D_MODEL
SB_HEADS = 16
SB_WIDTH = SB_HEADS * HEAD_DIM
SSM_WIDTH = MIX_WIDTH - SB_WIDTH
SSM_GROUP = 16
SSM_GROUPS = SSM_WIDTH // SSM_GROUP
SSM_STATE = 64
SSM_DT_MIN = 1e-3
SSM_DT_MAX = 1e-1
HGRN_HEADS = 16
HGRN_WIDTH = HGRN_HEADS * HEAD_DIM
HGRN_CHUNK = 64
MOBA_HEADS = 16
MOBA_WIDTH = MOBA_HEADS * HEAD_DIM
MOBA_BLOCK = 256
MOBA_TOPK = 3
ROPE_THETA = 500000.0
ROPE_DIM = HEAD_DIM // 4
Q_BLOCK = 128
D_FF = 11008
ADA_ROWS = 9
EVEN_IN = 3 * SB_WIDTH + SSM_WIDTH
ODD_IN = 4 * HGRN_WIDTH + 3 * MOBA_WIDTH
N_EVEN = (DEPTH + 1) // 2
N_ODD = DEPTH // 2
RMS_EPS = 1e-6
NEG = -1e30

kernel_name = "hybrid_sb_s5_hgrn2_moba_macaron"


def rms_norm(x, gain):
    xf = x.astype(jnp.float32)
    y = xf * lax.rsqrt(jnp.mean(xf * xf, axis=-1, keepdims=True) + RMS_EPS)
    return (y * gain.astype(jnp.float32)).astype(x.dtype)


def swiglu(h, w1, w3, w2):
    return (jax.nn.silu(h @ w1) * (h @ w3)) @ w2


def sandwich(x, fn, g_pre, g_post, shift, scale, gate, res_w):
    h = rms_norm(x, g_pre) * (1.0 + scale[:, None, :]) + shift[:, None, :]
    y = rms_norm(fn(h), g_post)
    return x + res_w * gate[:, None, :] * y


def partial_rope(t, pos):
    half = ROPE_DIM // 2
    inv = ROPE_THETA ** (-jnp.arange(half, dtype=jnp.float32) / half)
    ang = pos.astype(jnp.float32)[:, None] * inv[None, :]
    cos = jnp.cos(ang)[None, :, None, :]
    sin = jnp.sin(ang)[None, :, None, :]
    tr = t[..., :ROPE_DIM].astype(jnp.float32)
    t1, t2 = tr[..., :half], tr[..., half:]
    rot = jnp.concatenate([t1 * cos - t2 * sin, t2 * cos + t1 * sin], axis=-1)
    return jnp.concatenate([rot.astype(t.dtype), t[..., ROPE_DIM:]], axis=-1)


def stick_breaking_attention(q, k, v):
    B, H, S, Dh = q.shape
    scale = HEAD_DIM ** -0.5
    kpos = jnp.arange(S)

    def block(i):
        qb = lax.dynamic_slice_in_dim(q, i * Q_BLOCK, Q_BLOCK, axis=2)
        qpos = i * Q_BLOCK + jnp.arange(Q_BLOCK)
        z = jnp.einsum('bhtd,bhsd->bhts', qb, k).astype(jnp.float32) * scale
        mask = kpos[None, :] < qpos[:, None]
        log_keep = jnp.where(mask, jax.nn.log_sigmoid(-z), 0.0)
        tail = lax.cumsum(log_keep, axis=3, reverse=True)
        excl = jnp.concatenate([tail[..., 1:], jnp.zeros_like(tail[..., :1])], axis=3)
        w = jnp.where(mask, jnp.exp(jax.nn.log_sigmoid(z) + excl), 0.0)
        return jnp.einsum('bhts,bhsd->bhtd', w.astype(v.dtype), v)

    out = lax.map(block, jnp.arange(S // Q_BLOCK))
    return jnp.moveaxis(out, 0, 2).reshape(B, H, S, Dh)


def s5_mixer(u, a_re, a_im, b_re, b_im, c_re, c_im, d_skip, log_dt, glu_w):
    Bsz, S, _ = u.shape
    f32 = jnp.float32
    ug = u.reshape(Bsz, S, SSM_GROUPS, SSM_GROUP).astype(f32)
    dt = jnp.exp(log_dt.astype(f32))[:, None]
    ar, ai = a_re.astype(f32), a_im.astype(f32)
    mag = jnp.exp(ar * dt)
    lr, li = mag * jnp.cos(ai * dt), mag * jnp.sin(ai * dt)
    den = ar * ar + ai * ai
    nr, ni = lr - 1.0, li
    cr = (nr * ar + ni * ai) / den
    ci = (ni * ar - nr * ai) / den
    br, bi = b_re.astype(f32), b_im.astype(f32)
    bbr = cr[..., None] * br - ci[..., None] * bi
    bbi = cr[..., None] * bi + ci[..., None] * br
    xr = jnp.einsum('bsgh,gph->bsgp', ug, bbr)
    xi = jnp.einsum('bsgh,gph->bsgp', ug, bbi)
    lam_r = jnp.broadcast_to(lr, xr.shape)
    lam_i = jnp.broadcast_to(li, xr.shape)

    def combine(e1, e2):
        a1r, a1i, b1r, b1i = e1
        a2r, a2i, b2r, b2i = e2
        return (a2r * a1r - a2i * a1i, a2r * a1i + a2i * a1r,
                a2r * b1r - a2i * b1i + b2r, a2r * b1i + a2i * b1r + b2i)

    _, _, hr, hi = lax.associative_scan(combine, (lam_r, lam_i, xr, xi), axis=1)
    y = (jnp.einsum('gnp,bsgp->bsgn', c_re.astype(f32), hr)
         - jnp.einsum('gnp,bsgp->bsgn', c_im.astype(f32), hi))
    y = y + d_skip.astype(f32).reshape(SSM_GROUPS, SSM_GROUP) * ug
    y = jax.nn.gelu(y.reshape(Bsz, S, SSM_WIDTH))
    y = y * jax.nn.sigmoid(y @ glu_w.astype(f32))
    return y.astype(u.dtype)


def hgrn2_mixer(q, f_logit, i_in, gate, lb, norm_w):
    Bsz, S, _ = q.shape
    f32 = jnp.float32
    n_chunks = S // HGRN_CHUNK
    f = lb + (1.0 - lb) * jax.nn.sigmoid(f_logit.astype(f32))
    log_f = jnp.log(f)
    kk = 1.0 - f

    def to_chunks(t):
        t = t.astype(f32).reshape(Bsz, n_chunks, HGRN_CHUNK, HGRN_HEADS, HEAD_DIM)
        return t.transpose(1, 0, 3, 2, 4)

    qc, kc, gc, vc = to_chunks(q), to_chunks(kk), to_chunks(log_f), to_chunks(i_in)
    causal = jnp.tril(jnp.ones((HGRN_CHUNK, HGRN_CHUNK), dtype=bool))

    def step(state, inp):
        qb, kb, gb, vb = inp
        G = jnp.cumsum(gb, axis=2)
        o_inter = jnp.einsum('bhtk,bhkv->bhtv', qb * jnp.exp(G), state)
        diff = G[:, :, :, None, :] - G[:, :, None, :, :]
        decay = jnp.exp(jnp.where(causal[:, :, None], diff, -jnp.inf))
        att = jnp.einsum('bhtk,bhsk,bhtsk->bhts', qb, kb, decay)
        o = o_inter + jnp.einsum('bhts,bhsv->bhtv', att, vb)
        k_dec = kb * jnp.exp(G[:, :, -1:, :] - G)
        new_state = (state * jnp.exp(G[:, :, -1, :])[..., None]
                     + jnp.einsum('bhsk,bhsv->bhkv', k_dec, vb))
        return new_state, o

    init = jnp.zeros((Bsz, HGRN_HEADS, HEAD_DIM, HEAD_DIM), f32)
    _, o = lax.scan(step, init, (qc, kc, gc, vc))
    o = o.transpose(1, 0, 3, 2, 4).reshape(Bsz, S, HGRN_HEADS, HEAD_DIM)
    o = o * lax.rsqrt(jnp.mean(o * o, axis=-1, keepdims=True) + RMS_EPS) * norm_w.astype(f32)
    g = jax.nn.silu(gate.astype(f32)).reshape(Bsz, S, HGRN_HEADS, HEAD_DIM)
    return (o * g).reshape(Bsz, S, HGRN_WIDTH).astype(gate.dtype)


def moba_attention(q, k, v):
    B, H, S, Dh = q.shape
    nkb = -(-S // MOBA_BLOCK)
    pad = nkb * MOBA_BLOCK - S
    kp = jnp.pad(k, ((0, 0), (0, 0), (0, pad), (0, 0)))
    vp = jnp.pad(v, ((0, 0), (0, 0), (0, pad), (0, 0)))
    kblk = kp.reshape(B, H, nkb, MOBA_BLOCK, Dh)
    vblk = vp.reshape(B, H, nkb, MOBA_BLOCK, Dh)
    kmean = jnp.mean(kblk.astype(jnp.float32), axis=3)
    n_sel = max(1, min(MOBA_TOPK, nkb - 1))
    scale = HEAD_DIM ** -0.5
    b_idx = jnp.arange(B)[:, None, None]
    h_idx = jnp.arange(H)[None, :, None]

    def block(i):
        q0 = i * Q_BLOCK
        qb = lax.dynamic_slice_in_dim(q, q0, Q_BLOCK, axis=2)
        qpos = q0 + jnp.arange(Q_BLOCK)
        own = q0 // MOBA_BLOCK
        gate = jnp.einsum('bhtd,bhnd->bhtn', qb.astype(jnp.float32), kmean)
        past = jnp.arange(nkb) < own
        gate = jnp.where(past, gate, NEG)
        _, idx = lax.top_k(gate, n_sel)
        slot_ok = jnp.arange(n_sel) < own
        scores, vsel = [], []
        for j in range(n_sel):
            kg = kblk[b_idx, h_idx, idx[..., j]]
            sj = jnp.einsum('bhtd,bhtkd->bhtk', qb, kg).astype(jnp.float32) * scale
            scores.append(jnp.where(slot_ok[j], sj, NEG))
            vsel.append(vblk[b_idx, h_idx, idx[..., j]])
        ko = lax.dynamic_slice_in_dim(kp, own * MOBA_BLOCK, MOBA_BLOCK, axis=2)
        vo = lax.dynamic_slice_in_dim(vp, own * MOBA_BLOCK, MOBA_BLOCK, axis=2)
        kpos = own * MOBA_BLOCK + jnp.arange(MOBA_BLOCK)
        so = jnp.einsum('bhtd,bhkd->bhtk', qb, ko).astype(jnp.float32) * scale
        so = jnp.where(kpos[None, :] <= qpos[:, None], so, NEG)
        p = jax.nn.softmax(jnp.concatenate(scores + [so], axis=-1), axis=-1)
        p = p.reshape(B, H, Q_BLOCK, n_sel + 1, MOBA_BLOCK).astype(v.dtype)
        out = jnp.einsum('bhtk,bhkd->bhtd', p[:, :, :, n_sel, :], vo)
        for j in range(n_sel):
            out = out + jnp.einsum('bhtk,bhtkd->bhtd', p[:, :, :, j, :], vsel[j])
        return out

    out = lax.map(block, jnp.arange(S // Q_BLOCK))
    return jnp.moveaxis(out, 0, 2).reshape(B, H, S, Dh)


def even_mixer(h, w_in, w_out, a_re, a_im, b_re, b_im, c_re, c_im, d_skip, log_dt, glu_w):
    B, S, _ = h.shape
    proj = h @ w_in
    q, k, v, u = jnp.split(proj, [SB_WIDTH, 2 * SB_WIDTH, 3 * SB_WIDTH], axis=-1)
    heads = lambda t: t.reshape(B, S, SB_HEADS, HEAD_DIM).transpose(0, 2, 1, 3)
    o_a = stick_breaking_attention(heads(q), heads(k), heads(v))
    o_a = o_a.transpose(0, 2, 1, 3).reshape(B, S, SB_WIDTH)
    o_b = s5_mixer(u, a_re, a_im, b_re, b_im, c_re, c_im, d_skip, log_dt, glu_w)
    return jnp.concatenate([o_a, o_b], axis=-1) @ w_out


def odd_mixer(h, w_in, w_out, lb, norm_w):
    B, S, _ = h.shape
    proj = h @ w_in
    cuts = [HGRN_WIDTH, 2 * HGRN_WIDTH, 3 * HGRN_WIDTH, 4 * HGRN_WIDTH,
            4 * HGRN_WIDTH + MOBA_WIDTH, 4 * HGRN_WIDTH + 2 * MOBA_WIDTH]
    cq, cf, ci, cg, dq, dk, dv = jnp.split(proj, cuts, axis=-1)
    o_c = hgrn2_mixer(cq, cf, ci, cg, lb, norm_w)
    pos = jnp.arange(S)
    mh = lambda t: t.reshape(B, S, MOBA_HEADS, HEAD_DIM)
    qr = partial_rope(mh(dq), pos).transpose(0, 2, 1, 3)
    kr = partial_rope(mh(dk), pos).transpose(0, 2, 1, 3)
    vr = mh(dv).transpose(0, 2, 1, 3)
    o_d = moba_attention(qr, kr, vr).transpose(0, 2, 1, 3).reshape(B, S, MOBA_WIDTH)
    return jnp.concatenate([o_c, o_d.astype(o_c.dtype)], axis=-1) @ w_out


def setup_inputs(seed: int = 0) -> dict:
    key = jax.random.key(seed)
    ks = jax.random.split(key, 32)
    f32 = jnp.float32

    def nrm(k, shape, fan_in, gain=1.0):
        return jax.random.normal(k, shape, f32) * (gain * fan_in ** -0.5)

    x = jax.random.normal(ks[0], (BATCH, SEQ, D_MODEL), f32)
    c = jax.random.normal(ks[1], (BATCH, D_MODEL), f32)
    ada_w = nrm(ks[2], (D_MODEL, ADA_ROWS * D_MODEL), D_MODEL, 0.5)
    ada_table = 0.1 * jax.random.normal(ks[3], (DEPTH, ADA_ROWS, D_MODEL), f32)
    norm_pre = 1.0 + 0.05 * jax.random.normal(ks[4], (DEPTH, 3, D_MODEL), f32)
    norm_post = 1.0 + 0.05 * jax.random.normal(ks[5], (DEPTH, 3, D_MODEL), f32)
    ffn_w1 = nrm(ks[6], (DEPTH, 2, D_MODEL, D_FF), D_MODEL)
    ffn_w3 = nrm(ks[7], (DEPTH, 2, D_MODEL, D_FF), D_MODEL)
    ffn_w2 = nrm(ks[8], (DEPTH, 2, D_FF, D_MODEL), D_FF)
    ev_w_in = nrm(ks[9], (N_EVEN, D_MODEL, EVEN_IN), D_MODEL)
    ev_w_out = nrm(ks[10], (N_EVEN, MIX_WIDTH, D_MODEL), MIX_WIDTH)
    s5_a_re = -0.5 + 0.01 * jax.random.normal(ks[11], (N_EVEN, SSM_GROUPS, SSM_STATE), f32)
    s5_a_im = (math.pi * jnp.arange(SSM_STATE, dtype=f32)[None, None, :]
               + 0.01 * jax.random.normal(ks[12], (N_EVEN, SSM_GROUPS, SSM_STATE), f32))
    s5_b_re = nrm(ks[13], (N_EVEN, SSM_GROUPS, SSM_STATE, SSM_GROUP), 2 * SSM_GROUP)
    s5_b_im = nrm(ks[14], (N_EVEN, SSM_GROUPS, SSM_STATE, SSM_GROUP), 2 * SSM_GROUP)
    s5_c_re = nrm(ks[15], (N_EVEN, SSM_GROUPS, SSM_GROUP, SSM_STATE), SSM_STATE)
    s5_c_im = nrm(ks[16], (N_EVEN, SSM_GROUPS, SSM_GROUP, SSM_STATE), SSM_STATE)
    s5_d = jax.random.normal(ks[17], (N_EVEN, SSM_WIDTH), f32)
    s5_log_dt = jax.random.uniform(ks[18], (N_EVEN, SSM_GROUPS), f32,
                                   minval=math.log(SSM_DT_MIN), maxval=math.log(SSM_DT_MAX))
    s5_glu_w = nrm(ks[19], (N_EVEN, SSM_WIDTH, SSM_WIDTH), SSM_WIDTH)
    od_w_in = nrm(ks[20], (N_ODD, D_MODEL, ODD_IN), D_MODEL)
    od_w_out = nrm(ks[21], (N_ODD, MIX_WIDTH, D_MODEL), MIX_WIDTH)
    hgrn_lb = 0.5 * jax.random.normal(ks[22], (DEPTH, HGRN_WIDTH), f32)
    hgrn_norm_w = 1.0 + 0.05 * jax.random.normal(ks[23], (N_ODD, HEAD_DIM), f32)
    return {"x": x, "c": c, "ada_w": ada_w, "ada_table": ada_table,
            "norm_pre": norm_pre, "norm_post": norm_post,
            "ffn_w1": ffn_w1, "ffn_w3": ffn_w3, "ffn_w2": ffn_w2,
            "ev_w_in": ev_w_in, "ev_w_out": ev_w_out,
            "s5_a_re": s5_a_re, "s5_a_im": s5_a_im, "s5_b_re": s5_b_re, "s5_b_im": s5_b_im,
            "s5_c_re": s5_c_re, "s5_c_im": s5_c_im, "s5_d": s5_d, "s5_log_dt": s5_log_dt,
            "s5_glu_w": s5_glu_w, "od_w_in": od_w_in, "od_w_out": od_w_out,
            "hgrn_lb": hgrn_lb, "hgrn_norm_w": hgrn_norm_w}


def reference(x, c, ada_w, ada_table, norm_pre, norm_post, ffn_w1, ffn_w3, ffn_w2,
              ev_w_in, ev_w_out, s5_a_re, s5_a_im, s5_b_re, s5_b_im, s5_c_re, s5_c_im,
              s5_d, s5_log_dt, s5_glu_w, od_w_in, od_w_out, hgrn_lb, hgrn_norm_w):
    Bsz = x.shape[0]
    mod_shared = (jax.nn.silu(c) @ ada_w).reshape(Bsz, ADA_ROWS, D_MODEL)
    lb_cum = jnp.cumsum(jax.nn.softmax(hgrn_lb.astype(jnp.float32), axis=0), axis=0)
    lb_all = lb_cum - lb_cum[:1]
    for layer in range(DEPTH):
        mod = mod_shared + ada_table[layer][None]
        x = sandwich(x, lambda h: swiglu(h, ffn_w1[layer, 0], ffn_w3[layer, 0], ffn_w2[layer, 0]),
                     norm_pre[layer, 0], norm_post[layer, 0],
                     mod[:, 0], mod[:, 1], mod[:, 2], 0.5)
        if layer % 2 == 0:
            e = layer // 2
            mixer = lambda h: even_mixer(h, ev_w_in[e], ev_w_out[e], s5_a_re[e], s5_a_im[e],
                                         s5_b_re[e], s5_b_im[e], s5_c_re[e], s5_c_im[e],
                                         s5_d[e], s5_log_dt[e], s5_glu_w[e])
        else:
            o = layer // 2
            mixer = lambda h: odd_mixer(h, od_w_in[o], od_w_out[o], lb_all[layer], hgrn_norm_w[o])
        x = sandwich(x, mixer, norm_pre[layer, 1], norm_post[layer, 1],
                     mod[:, 3], mod[:, 4], mod[:, 5], 1.0)
        x = sandwich(x, lambda h: swiglu(h, ffn_w1[layer, 1], ffn_w3[layer, 1], ffn_w2[layer, 1]),
                     norm_pre[layer, 2], norm_post[layer, 2],
                     mod[:, 6], mod[:, 7], mod[:, 8], 0.5)
    return x
```

```python
import functools
import math

import jax
import jax.numpy as jnp
from jax import lax
from jax.experimental import pallas as pl
from jax.experimental.pallas import tpu as pltpu

F32 = jnp.float32
BF16 = jnp.bfloat16

D_MODEL = 4096
HEAD_DIM = 128
N_HEADS = 16
HALF_WIDTH = N_HEADS * HEAD_DIM
SSM_GROUP = 16
SSM_GROUPS = HALF_WIDTH // SSM_GROUP
SSM_STATE = 64
SSM_CHUNK = 16
HGRN_CHUNK = 64
HGRN_SUB = 16
MOBA_BLOCK = 256
MOBA_TOPK = 3
ROPE_THETA = 500000.0
ROPE_DIM = HEAD_DIM // 4
Q_BLOCK = 128
RMS_EPS = 1e-6
NEG = -1e30
EXP_UNDERFLOW = -104.0

VMEM_LIMIT = 56 * 1024 * 1024


def _params(*sem):
    return pltpu.CompilerParams(dimension_semantics=sem, vmem_limit_bytes=VMEM_LIMIT)


def _nt(a, b, **kw):
    return lax.dot_general(a, b, (((1,), (1,)), ((), ())), preferred_element_type=F32, **kw)


def _tn(a, b):
    return lax.dot_general(a, b, (((0,), (0,)), ((), ())), preferred_element_type=F32)


def _split_bf16(x):
    hi = x.astype(BF16)
    lo = (x - hi.astype(F32)).astype(BF16)
    return hi, lo


def _ada_kernel(c_ref, w_ref, o_ref):
    c = c_ref[...]
    a = c * jax.nn.sigmoid(c)
    o_ref[...] = jnp.dot(a, w_ref[...], precision=lax.Precision.HIGHEST,
                         preferred_element_type=F32)


def ada_project(c, ada_w, tn=1024):
    d, n = ada_w.shape
    bsz = c.shape[0]
    assert bsz <= 8
    c8 = jnp.zeros((8, d), F32).at[:bsz].set(c.astype(F32))
    out = pl.pallas_call(
        _ada_kernel,
        grid=(n // tn,),
        in_specs=[pl.BlockSpec((8, d), lambda j: (0, 0)),
                  pl.BlockSpec((d, tn), lambda j: (0, j))],
        out_specs=pl.BlockSpec((8, tn), lambda j: (0, j)),
        out_shape=jax.ShapeDtypeStruct((8, n), F32),
        compiler_params=_params("arbitrary"),
    )(c8, ada_w)
    return out[:bsz]


def _rms(x, gain):
    return x * lax.rsqrt(jnp.mean(x * x, axis=-1, keepdims=True) + RMS_EPS) * gain


def _pre_kernel(x_ref, vec_ref, h_ref):
    h = _rms(x_ref[...], vec_ref[0:1, :]) * (1.0 + vec_ref[1:2, :]) + vec_ref[2:3, :]
    h_ref[...] = h.astype(h_ref.dtype)


def pre_norm(x, g_pre, scale, shift, tm=256):
    m, d = x.shape
    vec = jnp.zeros((8, d), F32).at[0].set(g_pre).at[1].set(scale).at[2].set(shift)
    return pl.pallas_call(
        _pre_kernel,
        grid=(m // tm,),
        in_specs=[pl.BlockSpec((tm, d), lambda i: (i, 0)),
                  pl.BlockSpec((8, d), lambda i: (0, 0))],
        out_specs=pl.BlockSpec((tm, d), lambda i: (i, 0)),
        out_shape=jax.ShapeDtypeStruct((m, d), BF16),
        compiler_params=_params("arbitrary"),
    )(x, vec)


def _post_kernel(y_ref, x_ref, vec_ref, xo_ref, *maybe_h_ref, res_w):
    yn = _rms(y_ref[...].astype(F32), vec_ref[0:1, :])
    xn = x_ref[...] + (res_w * vec_ref[1:2, :]) * yn
    xo_ref[...] = xn
    if maybe_h_ref:
        h = _rms(xn, vec_ref[2:3, :]) * (1.0 + vec_ref[3:4, :]) + vec_ref[4:5, :]
        maybe_h_ref[0][...] = h.astype(BF16)


def post_norm(y, x, g_post, gate, res_w, nxt=None, tm=256):
    m, d = x.shape
    vec = jnp.zeros((8, d), F32).at[0].set(g_post).at[1].set(gate)
    out_shape = [jax.ShapeDtypeStruct((m, d), F32)]
    out_specs = [pl.BlockSpec((tm, d), lambda i: (i, 0))]
    if nxt is not None:
        vec = vec.at[2].set(nxt[0]).at[3].set(nxt[1]).at[4].set(nxt[2])
        out_shape.append(jax.ShapeDtypeStruct((m, d), BF16))
        out_specs.append(pl.BlockSpec((tm, d), lambda i: (i, 0)))
    res = pl.pallas_call(
        functools.partial(_post_kernel, res_w=res_w),
        grid=(m // tm,),
        in_specs=[pl.BlockSpec((tm, d), lambda i: (i, 0)),
                  pl.BlockSpec((tm, d), lambda i: (i, 0)),
                  pl.BlockSpec((8, d), lambda i: (0, 0))],
        out_specs=out_specs,
        out_shape=out_shape,
        compiler_params=_params("arbitrary"),
    )(y, x, vec)
    return (res[0], res[1]) if nxt is not None else (res[0], None)


def _mm_kernel(a_ref, w_ref, o_ref):
    o_ref[...] = jnp.dot(a_ref[...], w_ref[...].astype(BF16),
                         preferred_element_type=F32).astype(o_ref.dtype)


def matmul(a, w, out_dtype, tm=1024, tn=512):
    m, k = a.shape
    n = w.shape[1]
    tm, tn = min(tm, m), min(tn, n)
    return pl.pallas_call(
        _mm_kernel,
        grid=(m // tm, n // tn),
        in_specs=[pl.BlockSpec((tm, k), lambda i, j: (i, 0)),
                  pl.BlockSpec((k, tn), lambda i, j: (0, j))],
        out_specs=pl.BlockSpec((tm, tn), lambda i, j: (i, j)),
        out_shape=jax.ShapeDtypeStruct((m, n), out_dtype),
        compiler_params=_params("arbitrary", "arbitrary"),
    )(a, w)


def _gateup_kernel(a_ref, w1_ref, w3_ref, o_ref):
    a = a_ref[...]
    g = jnp.dot(a, w1_ref[...].astype(BF16), preferred_element_type=F32)
    u = jnp.dot(a, w3_ref[...].astype(BF16), preferred_element_type=F32)
    o_ref[...] = (g * jax.nn.sigmoid(g) * u).astype(o_ref.dtype)


def gate_up(a, w1, w3, tm=1024, tn=256):
    m, k = a.shape
    n = w1.shape[1]
    tm, tn = min(tm, m), min(tn, n)
    return pl.pallas_call(
        _gateup_kernel,
        grid=(m // tm, n // tn),
        in_specs=[pl.BlockSpec((tm, k), lambda i, j: (i, 0)),
                  pl.BlockSpec((k, tn), lambda i, j: (0, j)),
                  pl.BlockSpec((k, tn), lambda i, j: (0, j))],
        out_specs=pl.BlockSpec((tm, tn), lambda i, j: (i, j)),
        out_shape=jax.ShapeDtypeStruct((m, n), BF16),
        compiler_params=_params("arbitrary", "arbitrary"),
    )(a, w1, w3)


def _mm_acc_kernel(a_ref, w_ref, o_ref, *, tn):
    @pl.when(pl.program_id(1) == 0)
    def _():
        o_ref[...] = jnp.zeros_like(o_ref)

    a = a_ref[...]
    for n0 in range(0, o_ref.shape[1], tn):
        o_ref[:, n0:n0 + tn] += jnp.dot(a, w_ref[:, n0:n0 + tn].astype(BF16),
                                        preferred_element_type=F32)


def matmul_acc(a, w, tm=1024, tk=256):
    m, k = a.shape
    n = w.shape[1]
    tm, tk = min(tm, m), min(tk, k)
    return pl.pallas_call(
        functools.partial(_mm_acc_kernel, tn=min(512, n)),
        grid=(m // tm, k // tk),
        in_specs=[pl.BlockSpec((tm, tk), lambda i, kk: (i, kk)),
                  pl.BlockSpec((tk, n), lambda i, kk: (kk, 0))],
        out_specs=pl.BlockSpec((tm, n), lambda i, kk: (i, 0)),
        out_shape=jax.ShapeDtypeStruct((m, n), F32),
        compiler_params=_params("arbitrary", "arbitrary"),
    )(a, w)


def _sb_kernel(q_ref, k_ref, v_ref, o_ref):
    i = pl.program_id(1)
    t = Q_BLOCK
    q = q_ref[...]
    row = lax.broadcasted_iota(jnp.int32, (t, t), 0)
    col = lax.broadcasted_iota(jnp.int32, (t, t), 1)
    suffix = (row > col).astype(BF16)
    scale = HEAD_DIM ** -0.5

    def cond(carry):
        j, c, _ = carry
        return jnp.logical_and(j >= 0, jnp.max(c) >= EXP_UNDERFLOW)

    def body(carry):
        j, c, acc = carry
        start = pl.multiple_of(j * t, t)
        kj = k_ref[pl.ds(start, t), :]
        vj = v_ref[pl.ds(start, t), :]
        z = _nt(q, kj) * scale
        mask = (col - row) < (i - j) * t
        ls_pos = jnp.minimum(z, 0.0) - jnp.log(1.0 + jnp.exp(-jnp.abs(z)))
        lk = jnp.where(mask, ls_pos - z, 0.0)
        hi, lo = _split_bf16(lk)
        later = (jnp.dot(hi, suffix, preferred_element_type=F32)
                 + jnp.dot(lo, suffix, preferred_element_type=F32))
        w = jnp.where(mask, jnp.exp(ls_pos + later + c), 0.0)
        acc = acc + jnp.dot(w.astype(BF16), vj, preferred_element_type=F32)
        c = c + jnp.sum(lk, axis=1, keepdims=True)
        return j - 1, c, acc

    _, _, acc = lax.while_loop(
        cond, body, (i, jnp.zeros((t, 1), F32), jnp.zeros((t, HEAD_DIM), F32)))
    o_ref[...] = acc.astype(o_ref.dtype)


def stick_breaking(proj, q_off, k_off, v_off):
    s = proj.shape[0]
    return pl.pallas_call(
        _sb_kernel,
        grid=(N_HEADS, s // Q_BLOCK),
        in_specs=[pl.BlockSpec((Q_BLOCK, HEAD_DIM), lambda h, i: (i, q_off + h)),
                  pl.BlockSpec((s, HEAD_DIM), lambda h, i: (0, k_off + h)),
                  pl.BlockSpec((s, HEAD_DIM), lambda h, i: (0, v_off + h))],
        out_specs=pl.BlockSpec((Q_BLOCK, HEAD_DIM), lambda h, i: (i, h)),
        out_shape=jax.ShapeDtypeStruct((s, HALF_WIDTH), BF16),
        compiler_params=_params("arbitrary", "arbitrary"),
    )(proj, proj, proj)


def _s5_kernel(u_ref, toep_ref, win_ref, vout_ref, a1_ref, a2_ref, y_ref):
    u = u_ref[0]
    nc = u.shape[0]
    b = jnp.dot(u, win_ref[0], preferred_element_type=F32)
    row = lax.broadcasted_iota(jnp.int32, b.shape, 0)
    e = b
    for k in range(int(math.log2(nc))):
        sh = 1 << k
        es = jnp.where(row >= sh, pltpu.roll(e, sh, axis=0), 0.0)
        e = e + a1_ref[0, k:k + 1, :] * es + a2_ref[0, k:k + 1, :] * pltpu.roll(es, SSM_STATE, axis=1)
    s_in = jnp.where(row >= 1, pltpu.roll(e, 1, axis=0), 0.0)
    y = (jnp.dot(u, toep_ref[0], preferred_element_type=F32)
         + jnp.dot(s_in.astype(BF16), vout_ref[0], preferred_element_type=F32))
    y_ref[0] = y


def s5_tables(a_re, a_im, b_re, b_im, c_re, c_im, d_skip, log_dt):
    L, P, H, G = SSM_CHUNK, SSM_STATE, SSM_GROUP, SSM_GROUPS
    dt = jnp.exp(log_dt.astype(F32))[:, None]
    ar, ai = a_re.astype(F32), a_im.astype(F32)
    mag = jnp.exp(ar * dt)
    lr, li = mag * jnp.cos(ai * dt), mag * jnp.sin(ai * dt)
    den = ar * ar + ai * ai
    nr, ni = lr - 1.0, li
    cr = (nr * ar + ni * ai) / den
    ci = (ni * ar - nr * ai) / den
    br, bi = b_re.astype(F32), b_im.astype(F32)
    bbr = cr[..., None] * br - ci[..., None] * bi
    bbi = cr[..., None] * bi + ci[..., None] * br
    pr, pi = [jnp.ones_like(lr)], [jnp.zeros_like(li)]
    for _ in range(L):
        pr, pi = pr + [pr[-1] * lr - pi[-1] * li], pi + [pr[-1] * li + pi[-1] * lr]
    pr, pi = jnp.stack(pr, 1), jnp.stack(pi, 1)
    ccr, cci = c_re.astype(F32), c_im.astype(F32)
    clr = ccr[:, None] * pr[:, :L, None, :] - cci[:, None] * pi[:, :L, None, :]
    cli = ccr[:, None] * pi[:, :L, None, :] + cci[:, None] * pr[:, :L, None, :]
    kern = (jnp.einsum('gdnp,gph->gdnh', clr, bbr, precision='highest')
            - jnp.einsum('gdnp,gph->gdnh', cli, bbi, precision='highest'))
    kern = kern.at[:, 0].add(jnp.eye(H, dtype=F32)[None] * d_skip.astype(F32).reshape(G, H)[:, :, None])
    tau = jnp.arange(L)
    diff = tau[None, :] - tau[:, None]
    kt = kern[:, jnp.clip(diff, 0, L - 1)]
    kt = jnp.where((diff >= 0)[None, :, :, None, None], kt, 0.0)
    toep = kt.transpose(0, 1, 4, 2, 3).reshape(G, L * H, L * H)
    qr, qi = pr[:, L - 1 - tau], pi[:, L - 1 - tau]
    wre = qr[:, :, None, :] * bbr.transpose(0, 2, 1)[:, None] - qi[:, :, None, :] * bbi.transpose(0, 2, 1)[:, None]
    wim = qr[:, :, None, :] * bbi.transpose(0, 2, 1)[:, None] + qi[:, :, None, :] * bbr.transpose(0, 2, 1)[:, None]
    win = jnp.concatenate([wre, wim], -1).reshape(G, L * H, 2 * P)
    pr1, pi1 = pr[:, 1:], pi[:, 1:]
    vre = ccr[:, None] * pr1[:, :, None, :] - cci[:, None] * pi1[:, :, None, :]
    vim = ccr[:, None] * pi1[:, :, None, :] + cci[:, None] * pr1[:, :, None, :]
    vout = jnp.concatenate([vre.transpose(0, 3, 1, 2), -vim.transpose(0, 3, 1, 2)], 1).reshape(G, 2 * P, L * H)
    zr, zi = [pr[:, L]], [pi[:, L]]
    for _ in range(15):
        zr, zi = zr + [zr[-1] * zr[-1] - zi[-1] * zi[-1]], zi + [2.0 * zr[-1] * zi[-1]]
    zr, zi = jnp.stack(zr, 1), jnp.stack(zi, 1)
    a1 = jnp.concatenate([zr, zr], -1)
    a2 = jnp.concatenate([-zi, zi], -1)
    return toep.astype(BF16), win.astype(BF16), vout.astype(BF16), a1, a2


def s5_scan(u, tables):
    toep, win, vout, a1, a2 = tables
    s = u.shape[0]
    L, H, G, P = SSM_CHUNK, SSM_GROUP, SSM_GROUPS, SSM_STATE
    nc = s // L
    uc = u.reshape(nc, L, G, H).transpose(2, 0, 1, 3).reshape(G, nc, L * H)
    yc = pl.pallas_call(
        _s5_kernel,
        grid=(G,),
        in_specs=[pl.BlockSpec((1, nc, L * H), lambda g: (g, 0, 0)),
                  pl.BlockSpec((1, L * H, L * H), lambda g: (g, 0, 0)),
                  pl.BlockSpec((1, L * H, 2 * P), lambda g: (g, 0, 0)),
                  pl.BlockSpec((1, 2 * P, L * H), lambda g: (g, 0, 0)),
                  pl.BlockSpec((1, 16, 2 * P), lambda g: (g, 0, 0)),
                  pl.BlockSpec((1, 16, 2 * P), lambda g: (g, 0, 0))],
        out_specs=pl.BlockSpec((1, nc, L * H), lambda g: (g, 0, 0)),
        out_shape=jax.ShapeDtypeStruct((G, nc, L * H), F32),
        compiler_params=_params("arbitrary"),
    )(uc, toep, win, vout, a1, a2)
    return yc.reshape(G, nc, L, H).transpose(1, 2, 0, 3).reshape(s, G * H)


def _glu_kernel(y_ref, yj_ref, w_ref, o_ref):
    yg = jax.nn.gelu(y_ref[...])
    z = jnp.dot(yg.astype(BF16), w_ref[...].astype(BF16), preferred_element_type=F32)
    o_ref[...] = (jax.nn.gelu(yj_ref[...]) * jax.nn.sigmoid(z)).astype(o_ref.dtype)


def gelu_glu(y, glu_w, tm=512, tn=512):
    m, k = y.shape
    tm = min(tm, m)
    return pl.pallas_call(
        _glu_kernel,
        grid=(m // tm, k // tn),
        in_specs=[pl.BlockSpec((tm, k), lambda i, j: (i, 0)),
                  pl.BlockSpec((tm, tn), lambda i, j: (i, j)),
                  pl.BlockSpec((k, tn), lambda i, j: (0, j))],
        out_specs=pl.BlockSpec((tm, tn), lambda i, j: (i, j)),
        out_shape=jax.ShapeDtypeStruct((m, k), BF16),
        compiler_params=_params("arbitrary", "arbitrary"),
    )(y, y, glu_w)


def _hgrn_kernel(q_ref, f_ref, i_ref, g_ref, lb_ref, nw_ref, o_ref, state_ref):
    @pl.when(pl.program_id(1) == 0)
    def _():
        state_ref[...] = jnp.zeros_like(state_ref)

    c, sub = HGRN_CHUNK, HGRN_SUB
    lb = lb_ref[...]
    nw = nw_ref[...]
    r_i = lax.broadcasted_iota(jnp.int32, (c, c), 0)
    c_i = lax.broadcasted_iota(jnp.int32, (c, c), 1)
    tril = (c_i <= r_i).astype(BF16)
    s_idx = lax.broadcasted_iota(jnp.int32, (sub, HEAD_DIM), 0)

    def chunk(ci, _):
        r0 = pl.multiple_of(ci * c, c)
        qb = q_ref[pl.ds(r0, c), :]
        vb = i_ref[pl.ds(r0, c), :]
        f = lb + (1.0 - lb) * jax.nn.sigmoid(f_ref[pl.ds(r0, c), :])
        kb = 1.0 - f
        hi, lo = _split_bf16(jnp.log(f))
        gcum = (jnp.dot(tril, hi, preferred_element_type=F32)
                + jnp.dot(tril, lo, preferred_element_type=F32))
        gtot = gcum[c - 1:c, :]
        state = state_ref[...]
        o_inter = _nt((qb * jnp.exp(gcum)).astype(BF16), state.astype(BF16))
        vb16 = vb.astype(BF16)
        outs = []
        for blk in range(c // sub):
            r = blk * sub
            gs, qs, ks, vs = gcum[r:r + sub], qb[r:r + sub], kb[r:r + sub], vb[r:r + sub]
            gref = gcum[r:r + 1]
            rows = []
            for t in range(sub):
                d = jnp.where(s_idx <= t, gs[t:t + 1] - gs, -jnp.inf)
                a = jnp.sum(qs[t:t + 1] * ks * jnp.exp(d), axis=1, keepdims=True)
                rows.append(jnp.sum(a * vs, axis=0, keepdims=True))
            o_blk = jnp.concatenate(rows, axis=0)
            if blk:
                qd = (qs * jnp.exp(gs - gref)).astype(BF16)
                kd = (kb[:r] * jnp.exp(gref - gcum[:r])).astype(BF16)
                att = _nt(qd, kd)
                o_blk = o_blk + jnp.dot(att.astype(BF16), vb16[:r], preferred_element_type=F32)
            outs.append(o_blk)
        o = o_inter + jnp.concatenate(outs, axis=0)
        kdec = (kb * jnp.exp(gtot - gcum)).astype(BF16)
        state_ref[...] = state * jnp.exp(gtot) + _tn(vb16, kdec)
        on = o * lax.rsqrt(jnp.mean(o * o, axis=-1, keepdims=True) + RMS_EPS) * nw
        gate = g_ref[pl.ds(r0, c), :]
        o_ref[pl.ds(r0, c), :] = (on * (gate * jax.nn.sigmoid(gate))).astype(o_ref.dtype)
        return 0

    lax.fori_loop(0, q_ref.shape[0] // c, chunk, 0)


def hgrn2(proj, lb, norm_w, offs, tc=512):
    s = proj.shape[0]
    tc = min(tc, s)
    specs = [pl.BlockSpec((tc, HEAD_DIM), functools.partial(lambda h, i, o: (i, o + h), o=o)) for o in offs]
    return pl.pallas_call(
        _hgrn_kernel,
        grid=(N_HEADS, s // tc),
        in_specs=specs + [pl.BlockSpec((1, HEAD_DIM), lambda h, i: (0, h)),
                          pl.BlockSpec((1, HEAD_DIM), lambda h, i: (0, 0))],
        out_specs=pl.BlockSpec((tc, HEAD_DIM), lambda h, i: (i, h)),
        out_shape=jax.ShapeDtypeStruct((s, HALF_WIDTH), BF16),
        scratch_shapes=[pltpu.VMEM((HEAD_DIM, HEAD_DIM), F32)],
        compiler_params=_params("arbitrary", "arbitrary"),
    )(proj, proj, proj, proj, lb.reshape(1, HALF_WIDTH).astype(F32), norm_w.reshape(1, HEAD_DIM).astype(F32))


def rope_tables(s):
    half = ROPE_DIM // 2
    inv = ROPE_THETA ** (-jnp.arange(half, dtype=F32) / half)
    ang = jnp.arange(s).astype(F32)[:, None] * inv[None, :]
    cos, sin = jnp.cos(ang), jnp.sin(ang)
    one = jnp.ones((s, HEAD_DIM - ROPE_DIM), F32)
    zero = jnp.zeros((s, HEAD_DIM - ROPE_DIM), F32)
    z16 = jnp.zeros((s, half), F32)
    c = jnp.concatenate([cos, cos, one], 1)
    s_up = jnp.concatenate([z16, sin, zero], 1)
    s_dn = jnp.concatenate([-sin, z16, zero], 1)
    return c, s_up, s_dn


def _rope(x, c, s_up, s_dn):
    half = ROPE_DIM // 2
    return (x * c + pltpu.roll(x, half, axis=1) * s_up
            + pltpu.roll(x, HEAD_DIM - half, axis=1) * s_dn)


def _moba_prep_kernel(k_ref, v_ref, c_ref, su_ref, sd_ref, kr_ref, vo_ref, km_ref):
    c, su, sd = c_ref[...], su_ref[...], sd_ref[...]
    for h in range(N_HEADS):
        sl = slice(h * HEAD_DIM, (h + 1) * HEAD_DIM)
        kr = _rope(k_ref[:, sl], c, su, sd)
        kr_ref[:, sl] = kr.astype(BF16)
        km_ref[0, :, sl] = jnp.mean(kr, axis=0, keepdims=True)
    vo_ref[...] = v_ref[...].astype(BF16)


def moba_prep(proj, k_blk, v_blk, tables):
    s = proj.shape[0]
    nb = s // MOBA_BLOCK
    tab = pl.BlockSpec((MOBA_BLOCK, HEAD_DIM), lambda n: (n, 0))
    kr, v16, km = pl.pallas_call(
        _moba_prep_kernel,
        grid=(nb,),
        in_specs=[pl.BlockSpec((MOBA_BLOCK, HALF_WIDTH), lambda n: (n, k_blk)),
                  pl.BlockSpec((MOBA_BLOCK, HALF_WIDTH), lambda n: (n, v_blk)),
                  tab, tab, tab],
        out_specs=[pl.BlockSpec((MOBA_BLOCK, HALF_WIDTH), lambda n: (n, 0)),
                   pl.BlockSpec((MOBA_BLOCK, HALF_WIDTH), lambda n: (n, 0)),
                   pl.BlockSpec((1, 1, HALF_WIDTH), lambda n: (n, 0, 0))],
        out_shape=[jax.ShapeDtypeStruct((s, HALF_WIDTH), BF16),
                   jax.ShapeDtypeStruct((s, HALF_WIDTH), BF16),
                   jax.ShapeDtypeStruct((nb, 1, HALF_WIDTH), F32)],
        compiler_params=_params("arbitrary"),
    )(proj, proj, *tables)
    return kr, v16, km.reshape(nb, HALF_WIDTH)


def _moba_kernel(q_ref, c_ref, su_ref, sd_ref, k_ref, v_ref, km_ref, o_ref):
    i = pl.program_id(1)
    t, blk = Q_BLOCK, MOBA_BLOCK
    nb = km_ref.shape[0]
    own = (i * t) // blk
    scale = HEAD_DIM ** -0.5
    qf = _rope(q_ref[...], c_ref[...], su_ref[...], sd_ref[...])
    q = qf.astype(BF16)

    gate = _nt(qf, km_ref[...], precision=lax.Precision.HIGHEST)
    lane = lax.broadcasted_iota(jnp.int32, (t, nb), 1)
    g = jnp.where(lane < own, gate, NEG)
    sel = jnp.zeros((t, nb), F32)
    for j in range(MOBA_TOPK):
        m = jnp.max(g, axis=1, keepdims=True)
        idx = jnp.min(jnp.where(g == m, lane, nb), axis=1, keepdims=True)
        pick = lane == idx
        sel = jnp.where(pick, jnp.where(j < own, 1.0, 0.0), sel)
        g = jnp.where(pick, -jnp.inf, g)

    o0 = pl.multiple_of(own * blk, blk)
    s0 = _nt(q, k_ref[pl.ds(o0, blk), :]) * scale
    qpos = i * t + lax.broadcasted_iota(jnp.int32, (t, blk), 0)
    kpos = own * blk + lax.broadcasted_iota(jnp.int32, (t, blk), 1)
    s0 = jnp.where(kpos <= qpos, s0, NEG)
    m0 = jnp.max(s0, axis=1, keepdims=True)
    p0 = jnp.exp(s0 - m0)
    l0 = jnp.sum(p0, axis=1, keepdims=True)
    acc0 = jnp.dot(p0.astype(BF16), v_ref[pl.ds(o0, blk), :], preferred_element_type=F32)

    def body(n, carry):
        m, l, acc = carry
        start = pl.multiple_of(n * blk, blk)
        sn = _nt(q, k_ref[pl.ds(start, blk), :]) * scale
        chosen = jnp.sum(jnp.where(lane == n, sel, 0.0), axis=1, keepdims=True)
        sn = jnp.where(chosen > 0.0, sn, NEG)
        m_new = jnp.maximum(m, jnp.max(sn, axis=1, keepdims=True))
        alpha = jnp.exp(m - m_new)
        p = jnp.exp(sn - m_new)
        l = alpha * l + jnp.sum(p, axis=1, keepdims=True)
        acc = alpha * acc + jnp.dot(p.astype(BF16), v_ref[pl.ds(start, blk), :],
                                    preferred_element_type=F32)
        return m_new, l, acc

    _, l, acc = lax.fori_loop(0, own, body, (m0, l0, acc0))
    o_ref[...] = (acc / l).astype(o_ref.dtype)


def moba(proj, q_off, kr, v16, kmean, tables):
    s = proj.shape[0]
    nb = s // MOBA_BLOCK
    tab = pl.BlockSpec((Q_BLOCK, HEAD_DIM), lambda h, i: (i, 0))
    return pl.pallas_call(
        _moba_kernel,
        grid=(N_HEADS, s // Q_BLOCK),
        in_specs=[pl.BlockSpec((Q_BLOCK, HEAD_DIM), lambda h, i: (i, q_off + h)),
                  tab, tab, tab,
                  pl.BlockSpec((s, HEAD_DIM), lambda h, i: (0, h)),
                  pl.BlockSpec((s, HEAD_DIM), lambda h, i: (0, h)),
                  pl.BlockSpec((nb, HEAD_DIM), lambda h, i: (0, h))],
        out_specs=pl.BlockSpec((Q_BLOCK, HEAD_DIM), lambda h, i: (i, h)),
        out_shape=jax.ShapeDtypeStruct((s, HALF_WIDTH), BF16),
        compiler_params=_params("arbitrary", "arbitrary"),
    )(proj, *tables, kr, v16, kmean)


def _ffn(h, w1, w3, w2):
    return matmul_acc(gate_up(h, w1, w3), w2)


def _even_mixer(h, w_in, w_out, s5_params, glu_w):
    proj = matmul(h, w_in, BF16)
    o_a = stick_breaking(proj, 0, N_HEADS, 2 * N_HEADS)
    y = s5_scan(proj[:, 3 * HALF_WIDTH:], s5_tables(*s5_params))
    o_b = gelu_glu(y, glu_w)
    return matmul_acc(jnp.concatenate([o_a, o_b], axis=-1), w_out)


def _odd_mixer(h, w_in, w_out, lb, norm_w):
    s = h.shape[0]
    proj = matmul(h, w_in, F32)
    o_c = hgrn2(proj, lb, norm_w, (0, N_HEADS, 2 * N_HEADS, 3 * N_HEADS))
    tables = rope_tables(s)
    kr, v16, kmean = moba_prep(proj, 5, 6, tables)
    o_d = moba(proj, 4 * N_HEADS, kr, v16, kmean, tables)
    return matmul_acc(jnp.concatenate([o_c, o_d], axis=-1), w_out)


def kernel(x, c, ada_w, ada_table, norm_pre, norm_post, ffn_w1, ffn_w3, ffn_w2, ev_w_in, ev_w_out,
           s5_a_re, s5_a_im, s5_b_re, s5_b_im, s5_c_re, s5_c_im, s5_d, s5_log_dt, s5_glu_w,
           od_w_in, od_w_out, hgrn_lb, hgrn_norm_w):
    bsz, seq, d = x.shape
    depth = ada_table.shape[0]
    mod_shared = ada_project(c, ada_w).reshape(bsz, 9, d)
    lb_cum = jnp.cumsum(jax.nn.softmax(hgrn_lb.astype(F32), axis=0), axis=0)
    lb_all = lb_cum - lb_cum[:1]

    outs = []
    for b in range(bsz):
        xb = x[b]
        mods = [mod_shared[b] + ada_table[layer] for layer in range(depth)]
        subs = [(layer, slot) for layer in range(depth) for slot in range(3)]
        first = subs[0]
        h = pre_norm(xb, norm_pre[first[0], first[1]], mods[first[0]][1], mods[first[0]][0])
        for n, (layer, slot) in enumerate(subs):
            mod = mods[layer]
            if slot == 1:
                if layer % 2 == 0:
                    e = layer // 2
                    y = _even_mixer(h, ev_w_in[e], ev_w_out[e],
                                    (s5_a_re[e], s5_a_im[e], s5_b_re[e], s5_b_im[e], s5_c_re[e], s5_c_im[e],
                                     s5_d[e], s5_log_dt[e]), s5_glu_w[e])
                else:
                    o = layer // 2
                    y = _odd_mixer(h, od_w_in[o], od_w_out[o], lb_all[layer], hgrn_norm_w[o])
                res_w = 1.0
            else:
                f = slot // 2
                y = _ffn(h, ffn_w1[layer, f], ffn_w3[layer, f], ffn_w2[layer, f])
                res_w = 0.5
            nxt = None
            if n + 1 < len(subs):
                nl, ns = subs[n + 1]
                nxt = (norm_pre[nl, ns], mods[nl][3 * ns + 1], mods[nl][3 * ns])
            xb, h = post_norm(y, xb, norm_post[layer, slot], mod[3 * slot + 2], res_w, nxt)
        outs.append(xb)
    return jnp.stack(outs, axis=0)
```

```python
import functools
import math

import jax
import jax.numpy as jnp
from jax import lax
from jax.experimental import pallas as pl
from jax.experimental.pallas import tpu as pltpu

F32 = jnp.float32
BF16 = jnp.bfloat16

HEAD_DIM = 128
N_HEADS = 16
HALF_WIDTH = N_HEADS * HEAD_DIM
SSM_GROUP = 16
SSM_GROUPS = HALF_WIDTH // SSM_GROUP
SSM_STATE = 64
SSM_CHUNK = 16
SSM_PACK = HEAD_DIM // SSM_GROUP
HGRN_SUB = 16
SB_BLOCK = 256
MOBA_BLOCK = 256
MOBA_TOPK = 3
ROPE_THETA = 500000.0
ROPE_DIM = HEAD_DIM // 4
RMS_EPS = 1e-6
NEG = -1e30
EXP_UNDERFLOW = -104.0

VMEM_LIMIT = 56 * 1024 * 1024


def _params(*sem):
    return pltpu.CompilerParams(dimension_semantics=sem, vmem_limit_bytes=VMEM_LIMIT)


def _nt(a, b, **kw):
    return lax.dot_general(a, b, (((1,), (1,)), ((), ())), preferred_element_type=F32, **kw)


def _tn(a, b):
    return lax.dot_general(a, b, (((0,), (0,)), ((), ())), preferred_element_type=F32)


def _split_bf16(x, parts=2):
    out = []
    for _ in range(parts):
        p = x.astype(BF16)
        out.append(p)
        x = x - p.astype(F32)
    return out


def _wspec(lead, block, index_map):
    lead = tuple(lead)
    return pl.BlockSpec((None,) * len(lead) + tuple(block), lambda *g: lead + tuple(index_map(*g)))


def _ada_kernel(c_ref, w_ref, o_ref):
    c = c_ref[...]
    a = c * jax.nn.sigmoid(c)
    o_ref[...] = jnp.dot(a, w_ref[...], precision=lax.Precision.HIGHEST,
                         preferred_element_type=F32)


def ada_project(c, ada_w, tn=1024):
    d, n = ada_w.shape
    bsz = c.shape[0]
    assert bsz <= 8
    c8 = jnp.zeros((8, d), F32).at[:bsz].set(c.astype(F32))
    out = pl.pallas_call(
        _ada_kernel,
        grid=(n // tn,),
        in_specs=[pl.BlockSpec((8, d), lambda j: (0, 0)),
                  pl.BlockSpec((d, tn), lambda j: (0, j))],
        out_specs=pl.BlockSpec((8, tn), lambda j: (0, j)),
        out_shape=jax.ShapeDtypeStruct((8, n), F32),
        compiler_params=_params("arbitrary"),
        name="ada_project",
    )(c8, ada_w)
    return out[:bsz]


def _rms(x, gain):
    return x * lax.rsqrt(jnp.mean(x * x, axis=-1, keepdims=True) + RMS_EPS) * gain


def _pre_kernel(x_ref, vec_ref, h_ref):
    h = _rms(x_ref[...], vec_ref[0:1, :]) * (1.0 + vec_ref[1:2, :]) + vec_ref[2:3, :]
    h_ref[...] = h.astype(h_ref.dtype)


def pre_norm(x, g_pre, scale, shift, tm=256):
    m, d = x.shape
    vec = jnp.zeros((8, d), F32).at[0].set(g_pre).at[1].set(scale).at[2].set(shift)
    return pl.pallas_call(
        _pre_kernel,
        grid=(m // tm,),
        in_specs=[pl.BlockSpec((tm, d), lambda i: (i, 0)),
                  pl.BlockSpec((8, d), lambda i: (0, 0))],
        out_specs=pl.BlockSpec((tm, d), lambda i: (i, 0)),
        out_shape=jax.ShapeDtypeStruct((m, d), BF16),
        compiler_params=_params("arbitrary"),
        name="pre_norm",
    )(x, vec)


def _post_kernel(y_ref, x_ref, vec_ref, xo_ref, *maybe_h_ref, res_w):
    yn = _rms(y_ref[...].astype(F32), vec_ref[0:1, :])
    xn = x_ref[...] + (res_w * vec_ref[1:2, :]) * yn
    xo_ref[...] = xn
    if maybe_h_ref:
        h = _rms(xn, vec_ref[2:3, :]) * (1.0 + vec_ref[3:4, :]) + vec_ref[4:5, :]
        maybe_h_ref[0][...] = h.astype(BF16)


def post_norm(y, x, g_post, gate, res_w, nxt=None, tm=256):
    m, d = x.shape
    vec = jnp.zeros((8, d), F32).at[0].set(g_post).at[1].set(gate)
    out_shape = [jax.ShapeDtypeStruct((m, d), F32)]
    out_specs = [pl.BlockSpec((tm, d), lambda i: (i, 0))]
    if nxt is not None:
        vec = vec.at[2].set(nxt[0]).at[3].set(nxt[1]).at[4].set(nxt[2])
        out_shape.append(jax.ShapeDtypeStruct((m, d), BF16))
        out_specs.append(pl.BlockSpec((tm, d), lambda i: (i, 0)))
    res = pl.pallas_call(
        functools.partial(_post_kernel, res_w=res_w),
        grid=(m // tm,),
        in_specs=[pl.BlockSpec((tm, d), lambda i: (i, 0)),
                  pl.BlockSpec((tm, d), lambda i: (i, 0)),
                  pl.BlockSpec((8, d), lambda i: (0, 0))],
        out_specs=out_specs,
        out_shape=out_shape,
        compiler_params=_params("arbitrary"),
        name="post_norm",
    )(y, x, vec)
    return (res[0], res[1]) if nxt is not None else (res[0], None)


def _mm_kernel(a_ref, w_ref, o_ref):
    o_ref[...] = jnp.dot(a_ref[...], w_ref[...].astype(BF16),
                         preferred_element_type=F32).astype(o_ref.dtype)


def matmul(a, w, lead, out_dtype, col0=0, ncols=None, tm=1024, tn=512):
    m, k = a.shape
    n = w.shape[-1] - col0 if ncols is None else ncols
    tm, tn = min(tm, m), min(tn, n)
    assert col0 % tn == 0 and n % tn == 0 and m % tm == 0
    j0 = col0 // tn
    return pl.pallas_call(
        _mm_kernel,
        grid=(m // tm, n // tn),
        in_specs=[pl.BlockSpec((tm, k), lambda i, j: (i, 0)),
                  _wspec(lead, (k, tn), lambda i, j: (0, j + j0))],
        out_specs=pl.BlockSpec((tm, tn), lambda i, j: (i, j)),
        out_shape=jax.ShapeDtypeStruct((m, n), out_dtype),
        compiler_params=_params("arbitrary", "arbitrary"),
        name="matmul",
    )(a, w)


def _gateup_kernel(a_ref, w1_ref, w3_ref, o_ref):
    a = a_ref[...]
    g = jnp.dot(a, w1_ref[...].astype(BF16), preferred_element_type=F32)
    u = jnp.dot(a, w3_ref[...].astype(BF16), preferred_element_type=F32)
    o_ref[...] = (g * jax.nn.sigmoid(g) * u).astype(o_ref.dtype)


def gate_up(a, w1, w3, lead, tm=1024, tn=256):
    m, k = a.shape
    n = w1.shape[-1]
    tm, tn = min(tm, m), min(tn, n)
    wspec = _wspec(lead, (k, tn), lambda i, j: (0, j))
    return pl.pallas_call(
        _gateup_kernel,
        grid=(m // tm, n // tn),
        in_specs=[pl.BlockSpec((tm, k), lambda i, j: (i, 0)), wspec, wspec],
        out_specs=pl.BlockSpec((tm, tn), lambda i, j: (i, j)),
        out_shape=jax.ShapeDtypeStruct((m, n), BF16),
        compiler_params=_params("arbitrary", "arbitrary"),
        name="gate_up",
    )(a, w1, w3)


def _mm_acc_kernel(*refs, nks, tn):
    a_refs, w_ref, o_ref = refs[:len(nks)], refs[len(nks)], refs[len(nks) + 1]
    kk = pl.program_id(1)

    @pl.when(kk == 0)
    def _():
        o_ref[...] = jnp.zeros_like(o_ref)

    start = 0
    for a_ref, nk in zip(a_refs, nks):
        @pl.when(jnp.logical_and(kk >= start, kk < start + nk))
        def _(a_ref=a_ref):
            a = a_ref[...]
            for n0 in range(0, o_ref.shape[1], tn):
                o_ref[:, n0:n0 + tn] += jnp.dot(a, w_ref[:, n0:n0 + tn].astype(BF16),
                                                preferred_element_type=F32)
        start += nk


def matmul_acc(a_list, w, lead, tm=1024, tk=256):
    m = a_list[0].shape[0]
    n = w.shape[-1]
    tm = min(tm, m)
    nks = tuple(a.shape[1] // tk for a in a_list)
    starts = [sum(nks[:p]) for p in range(len(nks))]
    a_specs = [pl.BlockSpec((tm, tk), functools.partial(
        lambda i, kk, s0, nk: (i, jnp.clip(kk - s0, 0, nk - 1)), s0=s0, nk=nk))
        for s0, nk in zip(starts, nks)]
    return pl.pallas_call(
        functools.partial(_mm_acc_kernel, nks=nks, tn=min(512, n)),
        grid=(m // tm, sum(nks)),
        in_specs=a_specs + [_wspec(lead, (tk, n), lambda i, kk: (kk, 0))],
        out_specs=pl.BlockSpec((tm, n), lambda i, kk: (i, 0)),
        out_shape=jax.ShapeDtypeStruct((m, n), F32),
        compiler_params=_params("arbitrary", "arbitrary"),
        name="matmul_acc",
    )(*a_list, w)


def _sb_kernel(q_ref, k_ref, v_ref, o_ref, *, t):
    i = pl.program_id(1)
    nch = q_ref.shape[0] // t
    qs = [q_ref[ch * t:(ch + 1) * t, :] for ch in range(nch)]
    row = lax.broadcasted_iota(jnp.int32, (t, t), 0)
    col = lax.broadcasted_iota(jnp.int32, (t, t), 1)
    suffix = (row > col).astype(BF16)
    scale = HEAD_DIM ** -0.5

    def cond(carry):
        step, cs, _ = carry
        top = functools.reduce(jnp.maximum, [jnp.max(c) for c in cs])
        return jnp.logical_and(nch * i + nch - 1 - step >= 0, top >= EXP_UNDERFLOW)

    def body(carry):
        step, cs, accs = carry
        new_c, new_acc = [], []
        for ch in range(nch):
            qblk = nch * i + ch
            j = qblk - step
            jc = jnp.maximum(j, 0)
            start = pl.multiple_of(jc * t, t)
            kj = k_ref[pl.ds(start, t), :]
            vj = v_ref[pl.ds(start, t), :]
            z = _nt(qs[ch], kj) * scale
            mask = (col - row) < jnp.where(j >= 0, (qblk - jc) * t, -2 * t)
            ls_pos = jnp.minimum(z, 0.0) - jnp.log(1.0 + jnp.exp(-jnp.abs(z)))
            lk = jnp.where(mask, ls_pos - z, 0.0)
            later = sum(jnp.dot(p, suffix, preferred_element_type=F32) for p in _split_bf16(lk))
            w = jnp.where(mask, jnp.exp(ls_pos + later + cs[ch]), 0.0)
            new_acc.append(accs[ch] + jnp.dot(w.astype(BF16), vj, preferred_element_type=F32))
            new_c.append(cs[ch] + jnp.sum(lk, axis=1, keepdims=True))
        return step + 1, tuple(new_c), tuple(new_acc)

    init = (jnp.int32(0), tuple(jnp.zeros((t, 1), F32) for _ in range(nch)),
            tuple(jnp.zeros((t, HEAD_DIM), F32) for _ in range(nch)))
    _, _, accs = lax.while_loop(cond, body, init)
    for ch in range(nch):
        o_ref[ch * t:(ch + 1) * t, :] = accs[ch].astype(o_ref.dtype)


def stick_breaking(proj, q_off, k_off, v_off, chains=2):
    s = proj.shape[0]
    t = min(SB_BLOCK, s)
    tq = min(chains * t, s)
    return pl.pallas_call(
        functools.partial(_sb_kernel, t=t),
        grid=(N_HEADS, s // tq),
        in_specs=[pl.BlockSpec((tq, HEAD_DIM), lambda h, i: (i, q_off + h)),
                  pl.BlockSpec((s, HEAD_DIM), lambda h, i: (0, k_off + h)),
                  pl.BlockSpec((s, HEAD_DIM), lambda h, i: (0, v_off + h))],
        out_specs=pl.BlockSpec((tq, HEAD_DIM), lambda h, i: (i, h)),
        out_shape=jax.ShapeDtypeStruct((s, HALF_WIDTH), BF16),
        compiler_params=_params("arbitrary", "arbitrary"),
        name="stick_breaking",
    )(proj, proj, proj)


def _s5_kernel(u_ref, toep_ref, win_ref, vout_ref, ar_ref, ai_ref, y_ref, sin_ref, acc_ref):
    L = SSM_CHUNK
    nc = u_ref.shape[0] // L
    jc = pl.program_id(1)
    xp = [jnp.concatenate([u_ref[pl.ds(2 * p, nc, stride=L), :], u_ref[pl.ds(2 * p + 1, nc, stride=L), :]],
                          axis=1).astype(BF16) for p in range(L // 2)]
    pair = lambda ref, p: ref[0, 2 * p:2 * p + 2].reshape(2 * HEAD_DIM, ref.shape[-1])

    @pl.when(jc == 0)
    def _():
        b = sum(jnp.dot(xp[p], pair(win_ref, p), preferred_element_type=F32) for p in range(L // 2))
        half = b.shape[1] // 2
        er, ei = b[:, :half], b[:, half:]
        row = lax.broadcasted_iota(jnp.int32, er.shape, 0)
        for k in range(int(math.log2(nc))):
            sh = 1 << k
            sr = jnp.where(row >= sh, pltpu.roll(er, sh, axis=0), 0.0)
            si = jnp.where(row >= sh, pltpu.roll(ei, sh, axis=0), 0.0)
            ar, ai = ar_ref[0, k:k + 1, :], ai_ref[0, k:k + 1, :]
            er, ei = er + ar * sr - ai * si, ei + ar * si + ai * sr
        sin_ref[:, :half] = jnp.where(row >= 1, pltpu.roll(er, 1, axis=0), 0.0).astype(BF16)
        sin_ref[:, half:] = jnp.where(row >= 1, pltpu.roll(ei, 1, axis=0), 0.0).astype(BF16)

    per = acc_ref.shape[1] // HEAD_DIM
    acc_ref[...] = jnp.dot(sin_ref[...], vout_ref[0], preferred_element_type=F32)
    for p in range(L // 2):
        @pl.when(2 * p < (jc + 1) * per)
        def _(p=p):
            acc_ref[...] += jnp.dot(xp[p], pair(toep_ref, p), preferred_element_type=F32)
    for t in range(per):
        y_ref[pl.ds(jc * per + t, nc, stride=L), :] = acc_ref[:, t * HEAD_DIM:(t + 1) * HEAD_DIM]


def s5_tables(a_re, a_im, b_re, b_im, c_re, c_im, d_skip, log_dt):
    L, P, H, G, K8 = SSM_CHUNK, SSM_STATE, SSM_GROUP, SSM_GROUPS, SSM_PACK
    GC = G // K8
    dt = jnp.exp(log_dt.astype(F32))[:, None]
    ar, ai = a_re.astype(F32), a_im.astype(F32)
    mag = jnp.exp(ar * dt)
    lr, li = mag * jnp.cos(ai * dt), mag * jnp.sin(ai * dt)
    den = ar * ar + ai * ai
    nr, ni = lr - 1.0, li
    cr = (nr * ar + ni * ai) / den
    ci = (ni * ar - nr * ai) / den
    br, bi = b_re.astype(F32), b_im.astype(F32)
    bbr = cr[..., None] * br - ci[..., None] * bi
    bbi = cr[..., None] * bi + ci[..., None] * br
    pr, pi = [jnp.ones_like(lr)], [jnp.zeros_like(li)]
    for _ in range(L):
        pr, pi = pr + [pr[-1] * lr - pi[-1] * li], pi + [pr[-1] * li + pi[-1] * lr]
    pr, pi = jnp.stack(pr, 1), jnp.stack(pi, 1)
    ccr, cci = c_re.astype(F32), c_im.astype(F32)
    clr = ccr[:, None] * pr[:, :L, None, :] - cci[:, None] * pi[:, :L, None, :]
    cli = ccr[:, None] * pi[:, :L, None, :] + cci[:, None] * pr[:, :L, None, :]
    kern = (jnp.einsum('gdnp,gph->gdnh', clr, bbr, precision='highest')
            - jnp.einsum('gdnp,gph->gdnh', cli, bbi, precision='highest'))
    kern = kern.at[:, 0].add(jnp.eye(H, dtype=F32)[None] * d_skip.astype(F32).reshape(G, H)[:, :, None])
    tau = jnp.arange(L)
    diff = tau[None, :] - tau[:, None]
    kt = kern[:, jnp.clip(diff, 0, L - 1)]
    kt = jnp.where((diff >= 0)[None, :, :, None, None], kt, 0.0)
    eye = jnp.eye(K8, dtype=F32)
    toep = jnp.einsum('cgktnh,gj->ckghtjn', kt.reshape(GC, K8, L, L, H, H), eye)
    toep = toep.reshape(GC, L, K8 * H, L * K8 * H)
    qr, qi = pr[:, L - 1 - tau], pi[:, L - 1 - tau]
    bt_r, bt_i = bbr.transpose(0, 2, 1)[:, None], bbi.transpose(0, 2, 1)[:, None]
    wre = qr[:, :, None, :] * bt_r - qi[:, :, None, :] * bt_i
    wim = qr[:, :, None, :] * bt_i + qi[:, :, None, :] * bt_r
    win = jnp.stack([wre, wim], 3)
    win = jnp.einsum('cgkhrp,gj->ckghrjp', win.reshape(GC, K8, L, H, 2, P), eye)
    win = win.reshape(GC, L, K8 * H, 2 * K8 * P)
    pr1, pi1 = pr[:, 1:], pi[:, 1:]
    vre = ccr[:, None] * pr1[:, :, None, :] - cci[:, None] * pi1[:, :, None, :]
    vim = ccr[:, None] * pi1[:, :, None, :] + cci[:, None] * pr1[:, :, None, :]
    vout = jnp.stack([vre, -vim], 1).transpose(0, 1, 4, 2, 3)
    vout = jnp.einsum('cgrptn,gj->crgptjn', vout.reshape(GC, K8, 2, P, L, H), eye)
    vout = vout.reshape(GC, 2 * K8 * P, L * K8 * H)
    zr, zi = [pr[:, L]], [pi[:, L]]
    for _ in range(15):
        zr, zi = zr + [zr[-1] * zr[-1] - zi[-1] * zi[-1]], zi + [2.0 * zr[-1] * zi[-1]]
    pack = lambda z: jnp.stack(z, 1).reshape(GC, K8, 16, P).transpose(0, 2, 1, 3).reshape(GC, 16, K8 * P)
    return toep.astype(BF16), win.astype(BF16), vout.astype(BF16), pack(zr), pack(zi)


def s5_scan(u, tables, ncol=4):
    toep, win, vout, zr, zi = tables
    s, width = u.shape
    L = SSM_CHUNK
    gc = width // HEAD_DIM
    wide = toep.shape[-1]
    cw = wide // ncol
    nstate = win.shape[-1]
    return pl.pallas_call(
        _s5_kernel,
        grid=(gc, ncol),
        in_specs=[pl.BlockSpec((s, HEAD_DIM), lambda g, j: (0, g)),
                  pl.BlockSpec((1, L, HEAD_DIM, cw), lambda g, j: (g, 0, 0, j)),
                  pl.BlockSpec((1, L, HEAD_DIM, nstate), lambda g, j: (g, 0, 0, 0)),
                  pl.BlockSpec((1, nstate, cw), lambda g, j: (g, 0, j)),
                  pl.BlockSpec((1, 16, nstate // 2), lambda g, j: (g, 0, 0)),
                  pl.BlockSpec((1, 16, nstate // 2), lambda g, j: (g, 0, 0))],
        out_specs=pl.BlockSpec((s, HEAD_DIM), lambda g, j: (0, g)),
        out_shape=jax.ShapeDtypeStruct((s, width), F32),
        scratch_shapes=[pltpu.VMEM((s // L, nstate), BF16), pltpu.VMEM((s // L, cw), F32)],
        compiler_params=_params("arbitrary", "arbitrary"),
        name="s5_scan",
    )(u, toep, win, vout, zr, zi)


def _glu_kernel(y_ref, yj_ref, w_ref, o_ref):
    yg = jax.nn.gelu(y_ref[...])
    z = jnp.dot(yg.astype(BF16), w_ref[...].astype(BF16), preferred_element_type=F32)
    o_ref[...] = (jax.nn.gelu(yj_ref[...]) * jax.nn.sigmoid(z)).astype(o_ref.dtype)


def gelu_glu(y, glu_w, lead, tm=512, tn=512):
    m, k = y.shape
    tm = min(tm, m)
    return pl.pallas_call(
        _glu_kernel,
        grid=(m // tm, k // tn),
        in_specs=[pl.BlockSpec((tm, k), lambda i, j: (i, 0)),
                  pl.BlockSpec((tm, tn), lambda i, j: (i, j)),
                  _wspec(lead, (k, tn), lambda i, j: (0, j))],
        out_specs=pl.BlockSpec((tm, tn), lambda i, j: (i, j)),
        out_shape=jax.ShapeDtypeStruct((m, k), BF16),
        compiler_params=_params("arbitrary", "arbitrary"),
        name="gelu_glu",
    )(y, y, glu_w)


def _hgrn_kernel(q_ref, f_ref, i_ref, g_ref, lb_ref, nw_ref, o_ref, state_ref):
    @pl.when(pl.program_id(1) == 0)
    def _():
        state_ref[...] = jnp.zeros_like(state_ref)

    tc, d = q_ref.shape
    sub = HGRN_SUB
    nblk = tc // sub
    lb = lb_ref[...]
    q = q_ref[...]
    v = i_ref[...]
    f = lb + (1.0 - lb) * jax.nn.sigmoid(f_ref[...])
    k = 1.0 - f
    r_i = lax.broadcasted_iota(jnp.int32, (tc, tc), 0)
    c_i = lax.broadcasted_iota(jnp.int32, (tc, tc), 1)
    tril = (c_i <= r_i).astype(BF16)
    gcum = sum(jnp.dot(tril, p, preferred_element_type=F32)
               for p in _split_bf16(jnp.log(f), parts=3))
    gtot = gcum[tc - 1:tc, :]
    state = state_ref[...]
    o = _nt((q * jnp.exp(gcum)).astype(BF16), state.astype(BF16))

    g3, q3, k3, v3 = (x.reshape(nblk, sub, d) for x in (gcum, q, k, v))
    s_idx = lax.broadcasted_iota(jnp.int32, (nblk, sub, d), 1)
    rows = []
    for t in range(sub):
        dec = jnp.exp(jnp.where(s_idx <= t, g3[:, t:t + 1] - g3, -jnp.inf))
        a = jnp.sum(q3[:, t:t + 1] * k3 * dec, axis=2, keepdims=True)
        rows.append(jnp.sum(a * v3, axis=1, keepdims=True))
    o = o + jnp.concatenate(rows, axis=1).reshape(tc, d)

    v16 = v.astype(BF16)
    b = sub
    while 2 * b <= tc:
        m = tc // (2 * b)
        g4, q4, k4 = (x.reshape(m, 2 * b, d) for x in (gcum, q, k))
        ref = g4[:, b:b + 1]
        qd = (q4[:, b:] * jnp.exp(g4[:, b:] - ref)).astype(BF16)
        kd = (k4[:, :b] * jnp.exp(ref - g4[:, :b])).astype(BF16)
        att = jnp.einsum('mqd,mkd->mqk', qd, kd, preferred_element_type=F32)
        ob = jnp.einsum('mqk,mkd->mqd', att.astype(BF16), v16.reshape(m, 2 * b, d)[:, :b],
                        preferred_element_type=F32)
        o = o + jnp.concatenate([jnp.zeros_like(ob), ob], axis=1).reshape(tc, d)
        b *= 2

    kdec = (k * jnp.exp(gtot - gcum)).astype(BF16)
    state_ref[...] = state * jnp.exp(gtot) + _tn(v16, kdec)
    on = o * lax.rsqrt(jnp.mean(o * o, axis=-1, keepdims=True) + RMS_EPS) * nw_ref[...]
    gate = g_ref[...]
    o_ref[...] = (on * (gate * jax.nn.sigmoid(gate))).astype(o_ref.dtype)


def hgrn2(proj, lb, norm_w, offs, tc=512):
    s = proj.shape[0]
    tc = min(tc, s)
    specs = [pl.BlockSpec((tc, HEAD_DIM), functools.partial(lambda h, i, o: (i, o + h), o=o)) for o in offs]
    return pl.pallas_call(
        _hgrn_kernel,
        grid=(N_HEADS, s // tc),
        in_specs=specs + [pl.BlockSpec((1, HEAD_DIM), lambda h, i: (0, h)),
                          pl.BlockSpec((1, HEAD_DIM), lambda h, i: (0, 0))],
        out_specs=pl.BlockSpec((tc, HEAD_DIM), lambda h, i: (i, h)),
        out_shape=jax.ShapeDtypeStruct((s, HALF_WIDTH), BF16),
        scratch_shapes=[pltpu.VMEM((HEAD_DIM, HEAD_DIM), F32)],
        compiler_params=_params("arbitrary", "arbitrary"),
        name="hgrn2",
    )(proj, proj, proj, proj, lb.reshape(1, HALF_WIDTH).astype(F32), norm_w.reshape(1, HEAD_DIM).astype(F32))


def rope_tables(s):
    half = ROPE_DIM // 2
    inv = ROPE_THETA ** (-jnp.arange(half, dtype=F32) / half)
    ang = jnp.arange(s).astype(F32)[:, None] * inv[None, :]
    cos, sin = jnp.cos(ang), jnp.sin(ang)
    one = jnp.ones((s, HEAD_DIM - ROPE_DIM), F32)
    zero = jnp.zeros((s, HEAD_DIM - ROPE_DIM), F32)
    z16 = jnp.zeros((s, half), F32)
    c = jnp.concatenate([cos, cos, one], 1)
    s_up = jnp.concatenate([z16, sin, zero], 1)
    s_dn = jnp.concatenate([-sin, z16, zero], 1)
    return c, s_up, s_dn


def _rope(x, c, s_up, s_dn):
    half = ROPE_DIM // 2
    return (x * c + pltpu.roll(x, half, axis=1) * s_up
            + pltpu.roll(x, HEAD_DIM - half, axis=1) * s_dn)


def _moba_prep_kernel(k_ref, v_ref, c_ref, su_ref, sd_ref, kr_ref, vt_ref, km_ref):
    c, su, sd = c_ref[...], su_ref[...], sd_ref[...]
    for h in range(N_HEADS):
        sl = slice(h * HEAD_DIM, (h + 1) * HEAD_DIM)
        kr = _rope(k_ref[:, sl], c, su, sd)
        kr_ref[:, sl] = kr.astype(BF16)
        km_ref[0, :, sl] = jnp.mean(kr, axis=0, keepdims=True)
    vt_ref[0] = v_ref[...].T.astype(BF16)


def moba_prep(proj, k_blk, v_blk, tables):
    s = proj.shape[0]
    nb = s // MOBA_BLOCK
    tab = pl.BlockSpec((MOBA_BLOCK, HEAD_DIM), lambda n: (n, 0))
    kr, vt, km = pl.pallas_call(
        _moba_prep_kernel,
        grid=(nb,),
        in_specs=[pl.BlockSpec((MOBA_BLOCK, HALF_WIDTH), lambda n: (n, k_blk)),
                  pl.BlockSpec((MOBA_BLOCK, HALF_WIDTH), lambda n: (n, v_blk)),
                  tab, tab, tab],
        out_specs=[pl.BlockSpec((MOBA_BLOCK, HALF_WIDTH), lambda n: (n, 0)),
                   pl.BlockSpec((1, HALF_WIDTH, MOBA_BLOCK), lambda n: (n, 0, 0)),
                   pl.BlockSpec((1, 1, HALF_WIDTH), lambda n: (n, 0, 0))],
        out_shape=[jax.ShapeDtypeStruct((s, HALF_WIDTH), BF16),
                   jax.ShapeDtypeStruct((nb, HALF_WIDTH, MOBA_BLOCK), BF16),
                   jax.ShapeDtypeStruct((nb, 1, HALF_WIDTH), F32)],
        compiler_params=_params("arbitrary"),
        name="moba_prep",
    )(proj, proj, *tables)
    return kr, vt, km.reshape(nb, HALF_WIDTH)


def _moba_kernel(q_ref, c_ref, su_ref, sd_ref, k_ref, vt_ref, km_ref, o_ref, sel_ref, *, unroll):
    own = pl.program_id(1)
    blk = MOBA_BLOCK
    nb = km_ref.shape[0]
    qf = _rope(q_ref[...], c_ref[...], su_ref[...], sd_ref[...])
    q = (qf * HEAD_DIM ** -0.5).astype(BF16)

    gate = _nt(km_ref[...], qf, precision=lax.Precision.HIGHEST)
    row = lax.broadcasted_iota(jnp.int32, gate.shape, 0)
    g = jnp.where(row < own, gate, NEG)
    sel = jnp.zeros(gate.shape, F32)
    for j in range(MOBA_TOPK):
        m = jnp.max(g, axis=0, keepdims=True)
        idx = jnp.min(jnp.where(g == m, row, nb), axis=0, keepdims=True)
        pick = row == idx
        sel = jnp.where(pick, jnp.where(j < own, 1.0, 0.0), sel)
        g = jnp.where(pick, -jnp.inf, g)
    sel_ref[...] = sel

    o0 = pl.multiple_of(own * blk, blk)
    s0 = _nt(k_ref[pl.ds(o0, blk), :], q)
    kpos = lax.broadcasted_iota(jnp.int32, s0.shape, 0)
    qpos = lax.broadcasted_iota(jnp.int32, s0.shape, 1)
    s0 = jnp.where(kpos <= qpos, s0, NEG)
    m0 = jnp.max(s0, axis=0, keepdims=True)
    p0 = jnp.exp(s0 - m0)
    l0 = jnp.sum(p0, axis=0, keepdims=True)
    acc0 = jnp.dot(vt_ref[own], p0.astype(BF16), preferred_element_type=F32)

    def body(it, carry):
        m, l, acc = carry
        scores = []
        for u in range(unroll):
            n = it * unroll + u
            start = pl.multiple_of(n * blk, blk)
            sn = _nt(k_ref[pl.ds(start, blk), :], q)
            scores.append(jnp.where(sel_ref[pl.ds(n, 1), :] > 0.0, sn, NEG))
        m_new = functools.reduce(jnp.maximum, [jnp.max(sn, axis=0, keepdims=True) for sn in scores], m)
        alpha = jnp.exp(m - m_new)
        l, acc = alpha * l, alpha * acc
        for u, sn in enumerate(scores):
            p = jnp.exp(sn - m_new)
            l = l + jnp.sum(p, axis=0, keepdims=True)
            acc = acc + jnp.dot(vt_ref[it * unroll + u], p.astype(BF16), preferred_element_type=F32)
        return m_new, l, acc

    trips = (own + unroll - 1) // unroll
    _, l, acc = lax.fori_loop(0, trips, body, (m0, l0, acc0))
    o_ref[...] = (acc / l).T.astype(o_ref.dtype)


def moba(proj, q_off, kr, vt, kmean, tables, unroll=4):
    s = proj.shape[0]
    nb = s // MOBA_BLOCK
    assert nb % unroll == 0
    tab = pl.BlockSpec((MOBA_BLOCK, HEAD_DIM), lambda h, i: (i, 0))
    return pl.pallas_call(
        functools.partial(_moba_kernel, unroll=unroll),
        grid=(N_HEADS, nb),
        in_specs=[pl.BlockSpec((MOBA_BLOCK, HEAD_DIM), lambda h, i: (i, q_off + h)),
                  tab, tab, tab,
                  pl.BlockSpec((s, HEAD_DIM), lambda h, i: (0, h)),
                  pl.BlockSpec((nb, HEAD_DIM, MOBA_BLOCK), lambda h, i: (0, h, 0)),
                  pl.BlockSpec((nb, HEAD_DIM), lambda h, i: (0, h))],
        out_specs=pl.BlockSpec((MOBA_BLOCK, HEAD_DIM), lambda h, i: (i, h)),
        out_shape=jax.ShapeDtypeStruct((s, HALF_WIDTH), BF16),
        scratch_shapes=[pltpu.VMEM((nb, MOBA_BLOCK), F32)],
        compiler_params=_params("arbitrary", "arbitrary"),
        name="moba",
    )(proj, *tables, kr, vt, kmean)


def _even_mixer(h, w_in, w_out, e, s5_params, glu_w):
    qkv = matmul(h, w_in, (e,), BF16, col0=0, ncols=3 * HALF_WIDTH)
    u = matmul(h, w_in, (e,), F32, col0=3 * HALF_WIDTH, ncols=HALF_WIDTH)
    o_a = stick_breaking(qkv, 0, N_HEADS, 2 * N_HEADS)
    o_b = gelu_glu(s5_scan(u, s5_tables(*s5_params)), glu_w, (e,))
    return matmul_acc([o_a, o_b], w_out, (e,))


def _odd_mixer(h, w_in, w_out, o, lb, norm_w):
    s = h.shape[0]
    proj = matmul(h, w_in, (o,), F32)
    o_c = hgrn2(proj, lb, norm_w, (0, N_HEADS, 2 * N_HEADS, 3 * N_HEADS))
    tables = rope_tables(s)
    kr, vt, kmean = moba_prep(proj, 5, 6, tables)
    o_d = moba(proj, 4 * N_HEADS, kr, vt, kmean, tables)
    return matmul_acc([o_c, o_d], w_out, (o,))


def kernel(x, c, ada_w, ada_table, norm_pre, norm_post, ffn_w1, ffn_w3, ffn_w2, ev_w_in, ev_w_out,
           s5_a_re, s5_a_im, s5_b_re, s5_b_im, s5_c_re, s5_c_im, s5_d, s5_log_dt, s5_glu_w,
           od_w_in, od_w_out, hgrn_lb, hgrn_norm_w):
    bsz, seq, d = x.shape
    depth = ada_table.shape[0]
    mod_shared = ada_project(c, ada_w).reshape(bsz, 9, d)
    lb_cum = jnp.cumsum(jax.nn.softmax(hgrn_lb.astype(F32), axis=0), axis=0)
    lb_all = lb_cum - lb_cum[:1]

    outs = []
    for b in range(bsz):
        xb = x[b]
        mods = [mod_shared[b] + ada_table[layer] for layer in range(depth)]
        subs = [(layer, slot) for layer in range(depth) for slot in range(3)]
        first = subs[0]
        h = pre_norm(xb, norm_pre[first[0], first[1]], mods[first[0]][1], mods[first[0]][0])
        for n, (layer, slot) in enumerate(subs):
            mod = mods[layer]
            if slot == 1:
                if layer % 2 == 0:
                    e = layer // 2
                    y = _even_mixer(h, ev_w_in, ev_w_out, e,
                                    (s5_a_re[e], s5_a_im[e], s5_b_re[e], s5_b_im[e], s5_c_re[e], s5_c_im[e],
                                     s5_d[e], s5_log_dt[e]), s5_glu_w)
                else:
                    o = layer // 2
                    y = _odd_mixer(h, od_w_in, od_w_out, o, lb_all[layer], hgrn_norm_w[o])
                res_w = 1.0
            else:
                f = slot // 2
                g = gate_up(h, ffn_w1, ffn_w3, (layer, f))
                y = matmul_acc([g], ffn_w2, (layer, f))
                res_w = 0.5
            nxt = None
            if n + 1 < len(subs):
                nl, ns = subs[n + 1]
                nxt = (norm_pre[nl, ns], mods[nl][3 * ns + 1], mods[nl][3 * ns])
            xb, h = post_norm(y, xb, norm_post[layer, slot], mod[3 * slot + 2], res_w, nxt)
        outs.append(xb)
    return jnp.stack(outs, axis=0)
```

```python
import functools
import math

import jax
import jax.numpy as jnp
from jax import lax
from jax.experimental import pallas as pl
from jax.experimental.pallas import tpu as pltpu

F32 = jnp.float32
BF16 = jnp.bfloat16

HEAD_DIM = 128
N_HEADS = 16
HALF_WIDTH = N_HEADS * HEAD_DIM
SSM_GROUP = 16
SSM_GROUPS = HALF_WIDTH // SSM_GROUP
SSM_STATE = 64
SSM_CHUNK = 16
SSM_PACK = HEAD_DIM // SSM_GROUP
HGRN_SUB = 16
SB_BLOCK = 256
MOBA_BLOCK = 256
MOBA_TOPK = 3
ROPE_THETA = 500000.0
ROPE_DIM = HEAD_DIM // 4
RMS_EPS = 1e-6
NEG = -1e30
EXP_UNDERFLOW = -104.0

VMEM_LIMIT = 56 * 1024 * 1024


def _params(*sem):
    return pltpu.CompilerParams(dimension_semantics=sem, vmem_limit_bytes=VMEM_LIMIT)


def _nt(a, b, **kw):
    return lax.dot_general(a, b, (((1,), (1,)), ((), ())), preferred_element_type=F32, **kw)


def _tn(a, b):
    return lax.dot_general(a, b, (((0,), (0,)), ((), ())), preferred_element_type=F32)


def _split_bf16(x, parts=2):
    out = []
    for _ in range(parts):
        p = x.astype(BF16)
        out.append(p)
        x = x - p.astype(F32)
    return out


def _wspec(lead, block, index_map):
    lead = tuple(lead)
    return pl.BlockSpec((None,) * len(lead) + tuple(block), lambda *g: lead + tuple(index_map(*g)))


def _ada_kernel(c_ref, w_ref, o_ref):
    c = c_ref[...]
    a = c * jax.nn.sigmoid(c)
    o_ref[...] = jnp.dot(a, w_ref[...], precision=lax.Precision.HIGHEST,
                         preferred_element_type=F32)


def ada_project(c, ada_w, tn=1024):
    d, n = ada_w.shape
    bsz = c.shape[0]
    assert bsz <= 8
    c8 = jnp.zeros((8, d), F32).at[:bsz].set(c.astype(F32))
    out = pl.pallas_call(
        _ada_kernel,
        grid=(n // tn,),
        in_specs=[pl.BlockSpec((8, d), lambda j: (0, 0)),
                  pl.BlockSpec((d, tn), lambda j: (0, j))],
        out_specs=pl.BlockSpec((8, tn), lambda j: (0, j)),
        out_shape=jax.ShapeDtypeStruct((8, n), F32),
        compiler_params=_params("arbitrary"),
        name="ada_project",
    )(c8, ada_w)
    return out[:bsz]


def _rms(x, gain):
    return x * lax.rsqrt(jnp.mean(x * x, axis=-1, keepdims=True) + RMS_EPS) * gain


def _pre_kernel(x_ref, vec_ref, h_ref):
    h = _rms(x_ref[...], vec_ref[0:1, :]) * (1.0 + vec_ref[1:2, :]) + vec_ref[2:3, :]
    h_ref[...] = h.astype(h_ref.dtype)


def pre_norm(x, g_pre, scale, shift, tm=256):
    m, d = x.shape
    vec = jnp.zeros((8, d), F32).at[0].set(g_pre).at[1].set(scale).at[2].set(shift)
    return pl.pallas_call(
        _pre_kernel,
        grid=(m // tm,),
        in_specs=[pl.BlockSpec((tm, d), lambda i: (i, 0)),
                  pl.BlockSpec((8, d), lambda i: (0, 0))],
        out_specs=pl.BlockSpec((tm, d), lambda i: (i, 0)),
        out_shape=jax.ShapeDtypeStruct((m, d), BF16),
        compiler_params=_params("arbitrary"),
        name="pre_norm",
    )(x, vec)


def _post_kernel(y_ref, x_ref, vec_ref, xo_ref, *maybe_h_ref, res_w):
    yn = _rms(y_ref[...].astype(F32), vec_ref[0:1, :])
    xn = x_ref[...] + (res_w * vec_ref[1:2, :]) * yn
    xo_ref[...] = xn
    if maybe_h_ref:
        h = _rms(xn, vec_ref[2:3, :]) * (1.0 + vec_ref[3:4, :]) + vec_ref[4:5, :]
        maybe_h_ref[0][...] = h.astype(BF16)


def post_norm(y, x, g_post, gate, res_w, nxt=None, tm=256):
    m, d = x.shape
    vec = jnp.zeros((8, d), F32).at[0].set(g_post).at[1].set(gate)
    out_shape = [jax.ShapeDtypeStruct((m, d), F32)]
    out_specs = [pl.BlockSpec((tm, d), lambda i: (i, 0))]
    if nxt is not None:
        vec = vec.at[2].set(nxt[0]).at[3].set(nxt[1]).at[4].set(nxt[2])
        out_shape.append(jax.ShapeDtypeStruct((m, d), BF16))
        out_specs.append(pl.BlockSpec((tm, d), lambda i: (i, 0)))
    res = pl.pallas_call(
        functools.partial(_post_kernel, res_w=res_w),
        grid=(m // tm,),
        in_specs=[pl.BlockSpec((tm, d), lambda i: (i, 0)),
                  pl.BlockSpec((tm, d), lambda i: (i, 0)),
                  pl.BlockSpec((8, d), lambda i: (0, 0))],
        out_specs=out_specs,
        out_shape=out_shape,
        compiler_params=_params("arbitrary"),
        name="post_norm",
    )(y, x, vec)
    return (res[0], res[1]) if nxt is not None else (res[0], None)


def _mm_kernel(a_ref, w_ref, o_ref):
    o_ref[...] = jnp.dot(a_ref[...], w_ref[...].astype(BF16),
                         preferred_element_type=F32).astype(o_ref.dtype)


def matmul(a, w, lead, out_dtype, col0=0, ncols=None, tm=1024, tn=512):
    m, k = a.shape
    n = w.shape[-1] - col0 if ncols is None else ncols
    tm, tn = min(tm, m), min(tn, n)
    assert col0 % tn == 0 and n % tn == 0 and m % tm == 0
    j0 = col0 // tn
    return pl.pallas_call(
        _mm_kernel,
        grid=(m // tm, n // tn),
        in_specs=[pl.BlockSpec((tm, k), lambda i, j: (i, 0)),
                  _wspec(lead, (k, tn), lambda i, j: (0, j + j0))],
        out_specs=pl.BlockSpec((tm, tn), lambda i, j: (i, j)),
        out_shape=jax.ShapeDtypeStruct((m, n), out_dtype),
        compiler_params=_params("arbitrary", "arbitrary"),
        name="matmul",
    )(a, w)


def _gateup_kernel(a_ref, w1_ref, w3_ref, o_ref):
    a = a_ref[...]
    g = jnp.dot(a, w1_ref[...].astype(BF16), preferred_element_type=F32)
    u = jnp.dot(a, w3_ref[...].astype(BF16), preferred_element_type=F32)
    o_ref[...] = (g * jax.nn.sigmoid(g) * u).astype(o_ref.dtype)


def gate_up(a, w1, w3, lead, tm=1024, tn=256):
    m, k = a.shape
    n = w1.shape[-1]
    tm, tn = min(tm, m), min(tn, n)
    wspec = _wspec(lead, (k, tn), lambda i, j: (0, j))
    return pl.pallas_call(
        _gateup_kernel,
        grid=(m // tm, n // tn),
        in_specs=[pl.BlockSpec((tm, k), lambda i, j: (i, 0)), wspec, wspec],
        out_specs=pl.BlockSpec((tm, tn), lambda i, j: (i, j)),
        out_shape=jax.ShapeDtypeStruct((m, n), BF16),
        compiler_params=_params("arbitrary", "arbitrary"),
        name="gate_up",
    )(a, w1, w3)


def _mm_acc_kernel(*refs, nks, tails, tn):
    a_refs, w_ref, o_ref = refs[:len(nks)], refs[len(nks)], refs[len(nks) + 1]
    kk = pl.program_id(1)
    tk = w_ref.shape[0]

    @pl.when(kk == 0)
    def _():
        o_ref[...] = jnp.zeros_like(o_ref)

    def accumulate(a, valid):
        if valid < tk:
            a = jnp.where(lax.broadcasted_iota(jnp.int32, a.shape, 1) < valid, a, jnp.zeros_like(a))
        for n0 in range(0, o_ref.shape[1], tn):
            w = w_ref[:, n0:n0 + tn]
            if valid < tk:
                w = jnp.where(lax.broadcasted_iota(jnp.int32, w.shape, 0) < valid, w, 0.0)
            o_ref[:, n0:n0 + tn] += jnp.dot(a, w.astype(BF16), preferred_element_type=F32)

    start = 0
    for a_ref, nk, tail in zip(a_refs, nks, tails):
        local = kk - start
        if tail == tk:
            @pl.when(jnp.logical_and(local >= 0, local < nk))
            def _(a_ref=a_ref):
                accumulate(a_ref[...], tk)
        else:
            @pl.when(jnp.logical_and(local >= 0, local < nk - 1))
            def _(a_ref=a_ref):
                accumulate(a_ref[...], tk)

            @pl.when(local == nk - 1)
            def _(a_ref=a_ref, tail=tail):
                accumulate(a_ref[...], tail)
        start += nk


def matmul_acc(a_list, w, lead, tm=1024, tk=512):
    m = a_list[0].shape[0]
    n = w.shape[-1]
    tm = min(tm, m)
    assert all(a.shape[1] % tk == 0 for a in a_list[:-1])
    nks = tuple(pl.cdiv(a.shape[1], tk) for a in a_list)
    tails = tuple(a.shape[1] - (nk - 1) * tk for a, nk in zip(a_list, nks))
    starts = [sum(nks[:p]) for p in range(len(nks))]
    a_specs = [pl.BlockSpec((tm, tk), functools.partial(
        lambda i, kk, s0, nk: (i, jnp.clip(kk - s0, 0, nk - 1)), s0=s0, nk=nk))
        for s0, nk in zip(starts, nks)]
    return pl.pallas_call(
        functools.partial(_mm_acc_kernel, nks=nks, tails=tails, tn=min(512, n)),
        grid=(m // tm, sum(nks)),
        in_specs=a_specs + [_wspec(lead, (tk, n), lambda i, kk: (kk, 0))],
        out_specs=pl.BlockSpec((tm, n), lambda i, kk: (i, 0)),
        out_shape=jax.ShapeDtypeStruct((m, n), F32),
        compiler_params=_params("arbitrary", "arbitrary"),
        name="matmul_acc",
    )(*a_list, w)


def _sb_kernel(q_ref, k_ref, v_ref, o_ref, *, t):
    i = pl.program_id(1)
    nch = q_ref.shape[0] // t
    qs = [q_ref[ch * t:(ch + 1) * t, :] for ch in range(nch)]
    row = lax.broadcasted_iota(jnp.int32, (t, t), 0)
    col = lax.broadcasted_iota(jnp.int32, (t, t), 1)
    suffix = (row > col).astype(BF16)
    scale = HEAD_DIM ** -0.5

    def cond(carry):
        step, cs, _ = carry
        top = functools.reduce(jnp.maximum, [jnp.max(c) for c in cs])
        return jnp.logical_and(nch * i + nch - 1 - step >= 0, top >= EXP_UNDERFLOW)

    def body(carry):
        step, cs, accs = carry
        new_c, new_acc = [], []
        for ch in range(nch):
            qblk = nch * i + ch
            j = qblk - step
            jc = jnp.maximum(j, 0)
            start = pl.multiple_of(jc * t, t)
            kj = k_ref[pl.ds(start, t), :]
            vj = v_ref[pl.ds(start, t), :]
            z = _nt(qs[ch], kj) * scale
            mask = (col - row) < jnp.where(j >= 0, (qblk - jc) * t, -2 * t)
            ls_pos = jnp.minimum(z, 0.0) - jnp.log(1.0 + jnp.exp(-jnp.abs(z)))
            lk = jnp.where(mask, ls_pos - z, 0.0)
            later = sum(jnp.dot(p, suffix, preferred_element_type=F32) for p in _split_bf16(lk))
            w = jnp.where(mask, jnp.exp(ls_pos + later + cs[ch]), 0.0)
            new_acc.append(accs[ch] + jnp.dot(w.astype(BF16), vj, preferred_element_type=F32))
            new_c.append(cs[ch] + jnp.sum(lk, axis=1, keepdims=True))
        return step + 1, tuple(new_c), tuple(new_acc)

    init = (jnp.int32(0), tuple(jnp.zeros((t, 1), F32) for _ in range(nch)),
            tuple(jnp.zeros((t, HEAD_DIM), F32) for _ in range(nch)))
    _, _, accs = lax.while_loop(cond, body, init)
    for ch in range(nch):
        o_ref[ch * t:(ch + 1) * t, :] = accs[ch].astype(o_ref.dtype)


def stick_breaking(proj, q_off, k_off, v_off, chains=2):
    s = proj.shape[0]
    t = min(SB_BLOCK, s)
    tq = min(chains * t, s)
    return pl.pallas_call(
        functools.partial(_sb_kernel, t=t),
        grid=(N_HEADS, s // tq),
        in_specs=[pl.BlockSpec((tq, HEAD_DIM), lambda h, i: (i, q_off + h)),
                  pl.BlockSpec((s, HEAD_DIM), lambda h, i: (0, k_off + h)),
                  pl.BlockSpec((s, HEAD_DIM), lambda h, i: (0, v_off + h))],
        out_specs=pl.BlockSpec((tq, HEAD_DIM), lambda h, i: (i, h)),
        out_shape=jax.ShapeDtypeStruct((s, HALF_WIDTH), BF16),
        compiler_params=_params("arbitrary", "arbitrary"),
        name="stick_breaking",
    )(proj, proj, proj)


def _s5_spread(rows, cols, row_period, col_period, row_div, col_div):
    r = lax.broadcasted_iota(jnp.int32, (rows, cols), 0)
    c = lax.broadcasted_iota(jnp.int32, (rows, cols), 1)
    same = jnp.logical_and(r // row_div == c // col_div, r % row_period == c % col_period)
    return same.astype(BF16)


def _s5_expand(compact, spread, rows_per_group, lanes_per_group):
    full = jnp.dot(compact, spread, preferred_element_type=F32)
    r = lax.broadcasted_iota(jnp.int32, full.shape, 0)
    c = lax.broadcasted_iota(jnp.int32, full.shape, 1)
    keep = (r // rows_per_group) % SSM_PACK == (c // lanes_per_group) % SSM_PACK
    return jnp.where(keep, full, 0.0).astype(BF16)


def _s5_kernel(u_ref, toep_ref, win_ref, vout_ref, ar_ref, ai_ref, y_ref, sin_ref, acc_ref):
    L, H, P = SSM_CHUNK, SSM_GROUP, SSM_STATE
    nc = u_ref.shape[0] // L
    jc = pl.program_id(1)
    per = acc_ref.shape[1] // HEAD_DIM
    xp = [jnp.concatenate([u_ref[pl.ds(2 * p, nc, stride=L), :], u_ref[pl.ds(2 * p + 1, nc, stride=L), :]],
                          axis=1).astype(BF16) for p in range(L // 2)]
    pair = lambda ref, p: ref[0, 0, 2 * p:2 * p + 2].reshape(2 * HEAD_DIM, ref.shape[-1])

    @pl.when(jc == 0)
    def _():
        nstate = sin_ref.shape[1]
        half = nstate // 2
        spread = _s5_spread(2 * P, nstate, P, P, P, half)
        b = sum(jnp.dot(xp[p], _s5_expand(pair(win_ref, p), spread, H, P), preferred_element_type=F32)
                for p in range(L // 2))
        er, ei = b[:, :half], b[:, half:]
        row = lax.broadcasted_iota(jnp.int32, er.shape, 0)
        for k in range(int(math.log2(nc))):
            sh = 1 << k
            sr = jnp.where(row >= sh, pltpu.roll(er, sh, axis=0), 0.0)
            si = jnp.where(row >= sh, pltpu.roll(ei, sh, axis=0), 0.0)
            ar, ai = ar_ref[0, k:k + 1, :], ai_ref[0, k:k + 1, :]
            er, ei = er + ar * sr - ai * si, ei + ar * si + ai * sr
        sin_ref[:, :half] = jnp.where(row >= 1, pltpu.roll(er, 1, axis=0), 0.0).astype(BF16)
        sin_ref[:, half:] = jnp.where(row >= 1, pltpu.roll(ei, 1, axis=0), 0.0).astype(BF16)

    spread = _s5_spread(per * H, per * HEAD_DIM, H, H, H, HEAD_DIM)
    acc_ref[...] = jnp.dot(sin_ref[...], _s5_expand(vout_ref[0, 0], spread, P, H),
                           preferred_element_type=F32)
    for p in range(L // 2):
        @pl.when(2 * p < (jc + 1) * per)
        def _(p=p):
            acc_ref[...] += jnp.dot(xp[p], _s5_expand(pair(toep_ref, p), spread, H, H),
                                    preferred_element_type=F32)
    for t in range(per):
        y_ref[pl.ds(jc * per + t, nc, stride=L), :] = acc_ref[:, t * HEAD_DIM:(t + 1) * HEAD_DIM]


def s5_tables(a_re, a_im, b_re, b_im, c_re, c_im, d_skip, log_dt, ncol):
    L, P, H, G, K8 = SSM_CHUNK, SSM_STATE, SSM_GROUP, SSM_GROUPS, SSM_PACK
    GC = G // K8
    per = L // ncol
    dt = jnp.exp(log_dt.astype(F32))[:, None]
    ar, ai = a_re.astype(F32), a_im.astype(F32)
    mag = jnp.exp(ar * dt)
    lr, li = mag * jnp.cos(ai * dt), mag * jnp.sin(ai * dt)
    den = ar * ar + ai * ai
    nr, ni = lr - 1.0, li
    cr = (nr * ar + ni * ai) / den
    ci = (ni * ar - nr * ai) / den
    br, bi = b_re.astype(F32), b_im.astype(F32)
    bbr = cr[..., None] * br - ci[..., None] * bi
    bbi = cr[..., None] * bi + ci[..., None] * br
    pr, pi = [jnp.ones_like(lr)], [jnp.zeros_like(li)]
    for _ in range(L):
        pr, pi = pr + [pr[-1] * lr - pi[-1] * li], pi + [pr[-1] * li + pi[-1] * lr]
    pr, pi = jnp.stack(pr, 1), jnp.stack(pi, 1)
    ccr, cci = c_re.astype(F32), c_im.astype(F32)
    clr = ccr[:, None] * pr[:, :L, None, :] - cci[:, None] * pi[:, :L, None, :]
    cli = ccr[:, None] * pi[:, :L, None, :] + cci[:, None] * pr[:, :L, None, :]
    kern = (jnp.einsum('gdnp,gph->gdnh', clr, bbr, precision='highest')
            - jnp.einsum('gdnp,gph->gdnh', cli, bbi, precision='highest'))
    kern = kern.at[:, 0].add(jnp.eye(H, dtype=F32)[None] * d_skip.astype(F32).reshape(G, H)[:, :, None])
    tau = jnp.arange(L)
    diff = tau[None, :] - tau[:, None]
    kt = kern[:, jnp.clip(diff, 0, L - 1)]
    kt = jnp.where((diff >= 0)[None, :, :, None, None], kt, 0.0)
    toep = kt.reshape(GC, K8, L, ncol, per, H, H).transpose(0, 3, 2, 1, 6, 4, 5)
    toep = toep.reshape(GC, ncol, L, K8 * H, per * H)
    qr, qi = pr[:, L - 1 - tau], pi[:, L - 1 - tau]
    bt_r, bt_i = bbr.transpose(0, 2, 1)[:, None], bbi.transpose(0, 2, 1)[:, None]
    wre = qr[:, :, None, :] * bt_r - qi[:, :, None, :] * bt_i
    wim = qr[:, :, None, :] * bt_i + qi[:, :, None, :] * bt_r
    win = jnp.concatenate([wre, wim], -1)
    win = win.reshape(GC, K8, L, H, 2 * P).transpose(0, 2, 1, 3, 4).reshape(GC, 1, L, K8 * H, 2 * P)
    pr1, pi1 = pr[:, 1:], pi[:, 1:]
    vre = ccr[:, None] * pr1[:, :, None, :] - cci[:, None] * pi1[:, :, None, :]
    vim = ccr[:, None] * pi1[:, :, None, :] + cci[:, None] * pr1[:, :, None, :]
    vout = jnp.stack([vre, -vim], 1)
    vout = vout.reshape(GC, K8, 2, ncol, per, H, P).transpose(0, 3, 2, 1, 6, 4, 5)
    vout = vout.reshape(GC, ncol, 2 * K8 * P, per * H)
    zr, zi = [pr[:, L]], [pi[:, L]]
    for _ in range(15):
        zr, zi = zr + [zr[-1] * zr[-1] - zi[-1] * zi[-1]], zi + [2.0 * zr[-1] * zi[-1]]
    pack = lambda z: jnp.stack(z, 1).reshape(GC, K8, 16, P).transpose(0, 2, 1, 3).reshape(GC, 16, K8 * P)
    return toep.astype(BF16), win.astype(BF16), vout.astype(BF16), pack(zr), pack(zi)


def s5_scan(u, s5_params, ncol=4):
    toep, win, vout, zr, zi = s5_tables(*s5_params, ncol=ncol)
    s, width = u.shape
    L, K8, P = SSM_CHUNK, SSM_PACK, SSM_STATE
    gc = width // HEAD_DIM
    cw = (L // ncol) * HEAD_DIM
    nstate = 2 * K8 * P
    return pl.pallas_call(
        _s5_kernel,
        grid=(gc, ncol),
        in_specs=[pl.BlockSpec((s, HEAD_DIM), lambda g, j: (0, g)),
                  pl.BlockSpec((1, 1) + toep.shape[2:], lambda g, j: (g, j, 0, 0, 0)),
                  pl.BlockSpec((1, 1) + win.shape[2:], lambda g, j: (g, 0, 0, 0, 0)),
                  pl.BlockSpec((1, 1) + vout.shape[2:], lambda g, j: (g, j, 0, 0)),
                  pl.BlockSpec((1, 16, nstate // 2), lambda g, j: (g, 0, 0)),
                  pl.BlockSpec((1, 16, nstate // 2), lambda g, j: (g, 0, 0))],
        out_specs=pl.BlockSpec((s, HEAD_DIM), lambda g, j: (0, g)),
        out_shape=jax.ShapeDtypeStruct((s, width), F32),
        scratch_shapes=[pltpu.VMEM((s // L, nstate), BF16), pltpu.VMEM((s // L, cw), F32)],
        compiler_params=_params("arbitrary", "arbitrary"),
        name="s5_scan",
    )(u, toep, win, vout, zr, zi)


def _glu_kernel(y_ref, yj_ref, w_ref, o_ref):
    yg = jax.nn.gelu(y_ref[...])
    z = jnp.dot(yg.astype(BF16), w_ref[...].astype(BF16), preferred_element_type=F32)
    o_ref[...] = (jax.nn.gelu(yj_ref[...]) * jax.nn.sigmoid(z)).astype(o_ref.dtype)


def gelu_glu(y, glu_w, lead, tm=512, tn=512):
    m, k = y.shape
    tm = min(tm, m)
    return pl.pallas_call(
        _glu_kernel,
        grid=(m // tm, k // tn),
        in_specs=[pl.BlockSpec((tm, k), lambda i, j: (i, 0)),
                  pl.BlockSpec((tm, tn), lambda i, j: (i, j)),
                  _wspec(lead, (k, tn), lambda i, j: (0, j))],
        out_specs=pl.BlockSpec((tm, tn), lambda i, j: (i, j)),
        out_shape=jax.ShapeDtypeStruct((m, k), BF16),
        compiler_params=_params("arbitrary", "arbitrary"),
        name="gelu_glu",
    )(y, y, glu_w)


def _hgrn_kernel(q_ref, f_ref, i_ref, g_ref, lb_ref, nw_ref, o_ref, state_ref):
    @pl.when(pl.program_id(1) == 0)
    def _():
        state_ref[...] = jnp.zeros_like(state_ref)

    tc, d = q_ref.shape
    sub = HGRN_SUB
    nblk = tc // sub
    lb = lb_ref[...]
    q = q_ref[...]
    v = i_ref[...]
    f = lb + (1.0 - lb) * jax.nn.sigmoid(f_ref[...])
    k = 1.0 - f
    r_i = lax.broadcasted_iota(jnp.int32, (tc, tc), 0)
    c_i = lax.broadcasted_iota(jnp.int32, (tc, tc), 1)
    tril = (c_i <= r_i).astype(BF16)
    gcum = sum(jnp.dot(tril, p, preferred_element_type=F32)
               for p in _split_bf16(jnp.log(f), parts=3))
    gtot = gcum[tc - 1:tc, :]
    state = state_ref[...]
    o = _nt((q * jnp.exp(gcum)).astype(BF16), state.astype(BF16))

    g3, q3, k3, v3 = (x.reshape(nblk, sub, d) for x in (gcum, q, k, v))
    s_idx = lax.broadcasted_iota(jnp.int32, (nblk, sub, d), 1)
    rows = []
    for t in range(sub):
        dec = jnp.exp(jnp.where(s_idx <= t, g3[:, t:t + 1] - g3, -jnp.inf))
        a = jnp.sum(q3[:, t:t + 1] * k3 * dec, axis=2, keepdims=True)
        rows.append(jnp.sum(a * v3, axis=1, keepdims=True))
    o = o + jnp.concatenate(rows, axis=1).reshape(tc, d)

    v16 = v.astype(BF16)
    b = sub
    while 2 * b <= tc:
        m = tc // (2 * b)
        g4, q4, k4 = (x.reshape(m, 2 * b, d) for x in (gcum, q, k))
        ref = g4[:, b:b + 1]
        qd = (q4[:, b:] * jnp.exp(g4[:, b:] - ref)).astype(BF16)
        kd = (k4[:, :b] * jnp.exp(ref - g4[:, :b])).astype(BF16)
        att = jnp.einsum('mqd,mkd->mqk', qd, kd, preferred_element_type=F32)
        ob = jnp.einsum('mqk,mkd->mqd', att.astype(BF16), v16.reshape(m, 2 * b, d)[:, :b],
                        preferred_element_type=F32)
        o = o + jnp.concatenate([jnp.zeros_like(ob), ob], axis=1).reshape(tc, d)
        b *= 2

    kdec = (k * jnp.exp(gtot - gcum)).astype(BF16)
    state_ref[...] = state * jnp.exp(gtot) + _tn(v16, kdec)
    on = o * lax.rsqrt(jnp.mean(o * o, axis=-1, keepdims=True) + RMS_EPS) * nw_ref[...]
    gate = g_ref[...]
    o_ref[...] = (on * (gate * jax.nn.sigmoid(gate))).astype(o_ref.dtype)


def hgrn2(proj, lb, norm_w, offs, tc=512):
    s = proj.shape[0]
    tc = min(tc, s)
    specs = [pl.BlockSpec((tc, HEAD_DIM), functools.partial(lambda h, i, o: (i, o + h), o=o)) for o in offs]
    return pl.pallas_call(
        _hgrn_kernel,
        grid=(N_HEADS, s // tc),
        in_specs=specs + [pl.BlockSpec((1, HEAD_DIM), lambda h, i: (0, h)),
                          pl.BlockSpec((1, HEAD_DIM), lambda h, i: (0, 0))],
        out_specs=pl.BlockSpec((tc, HEAD_DIM), lambda h, i: (i, h)),
        out_shape=jax.ShapeDtypeStruct((s, HALF_WIDTH), BF16),
        scratch_shapes=[pltpu.VMEM((HEAD_DIM, HEAD_DIM), F32)],
        compiler_params=_params("arbitrary", "arbitrary"),
        name="hgrn2",
    )(proj, proj, proj, proj, lb.reshape(1, HALF_WIDTH).astype(F32), norm_w.reshape(1, HEAD_DIM).astype(F32))


def rope_tables(s):
    half = ROPE_DIM // 2
    inv = ROPE_THETA ** (-jnp.arange(half, dtype=F32) / half)
    ang = jnp.arange(s).astype(F32)[:, None] * inv[None, :]
    cos, sin = jnp.cos(ang), jnp.sin(ang)
    one = jnp.ones((s, HEAD_DIM - ROPE_DIM), F32)
    zero = jnp.zeros((s, HEAD_DIM - ROPE_DIM), F32)
    z16 = jnp.zeros((s, half), F32)
    c = jnp.concatenate([cos, cos, one], 1)
    s_up = jnp.concatenate([z16, sin, zero], 1)
    s_dn = jnp.concatenate([-sin, z16, zero], 1)
    return c, s_up, s_dn


def _rope(x, c, s_up, s_dn):
    half = ROPE_DIM // 2
    return (x * c + pltpu.roll(x, half, axis=1) * s_up
            + pltpu.roll(x, HEAD_DIM - half, axis=1) * s_dn)


def _moba_prep_kernel(k_ref, v_ref, c_ref, su_ref, sd_ref, kr_ref, vt_ref, km_ref):
    c, su, sd = c_ref[...], su_ref[...], sd_ref[...]
    for h in range(N_HEADS):
        sl = slice(h * HEAD_DIM, (h + 1) * HEAD_DIM)
        kr = _rope(k_ref[:, sl], c, su, sd)
        kr_ref[:, sl] = kr.astype(BF16)
        km_ref[0, :, sl] = jnp.mean(kr, axis=0, keepdims=True)
    vt_ref[0] = v_ref[...].T.astype(BF16)


def moba_prep(proj, k_blk, v_blk, tables):
    s = proj.shape[0]
    nb = s // MOBA_BLOCK
    tab = pl.BlockSpec((MOBA_BLOCK, HEAD_DIM), lambda n: (n, 0))
    kr, vt, km = pl.pallas_call(
        _moba_prep_kernel,
        grid=(nb,),
        in_specs=[pl.BlockSpec((MOBA_BLOCK, HALF_WIDTH), lambda n: (n, k_blk)),
                  pl.BlockSpec((MOBA_BLOCK, HALF_WIDTH), lambda n: (n, v_blk)),
                  tab, tab, tab],
        out_specs=[pl.BlockSpec((MOBA_BLOCK, HALF_WIDTH), lambda n: (n, 0)),
                   pl.BlockSpec((1, HALF_WIDTH, MOBA_BLOCK), lambda n: (n, 0, 0)),
                   pl.BlockSpec((1, 1, HALF_WIDTH), lambda n: (n, 0, 0))],
        out_shape=[jax.ShapeDtypeStruct((s, HALF_WIDTH), BF16),
                   jax.ShapeDtypeStruct((nb, HALF_WIDTH, MOBA_BLOCK), BF16),
                   jax.ShapeDtypeStruct((nb, 1, HALF_WIDTH), F32)],
        compiler_params=_params("arbitrary"),
        name="moba_prep",
    )(proj, proj, *tables)
    return kr, vt, km.reshape(nb, HALF_WIDTH)


def _moba_kernel(q_ref, c_ref, su_ref, sd_ref, k_ref, vt_ref, km_ref, o_ref, sel_ref, *, unroll):
    own = pl.program_id(1)
    blk = MOBA_BLOCK
    nb = km_ref.shape[0]
    qf = _rope(q_ref[...], c_ref[...], su_ref[...], sd_ref[...])
    q = (qf * HEAD_DIM ** -0.5).astype(BF16)

    gate = _nt(km_ref[...], qf, precision=lax.Precision.HIGHEST)
    row = lax.broadcasted_iota(jnp.int32, gate.shape, 0)
    g = jnp.where(row < own, gate, NEG)
    sel = jnp.zeros(gate.shape, F32)
    for j in range(MOBA_TOPK):
        m = jnp.max(g, axis=0, keepdims=True)
        idx = jnp.min(jnp.where(g == m, row, nb), axis=0, keepdims=True)
        pick = row == idx
        sel = jnp.where(pick, jnp.where(j < own, 1.0, 0.0), sel)
        g = jnp.where(pick, -jnp.inf, g)
    sel_ref[...] = sel

    o0 = pl.multiple_of(own * blk, blk)
    s0 = _nt(k_ref[pl.ds(o0, blk), :], q)
    kpos = lax.broadcasted_iota(jnp.int32, s0.shape, 0)
    qpos = lax.broadcasted_iota(jnp.int32, s0.shape, 1)
    s0 = jnp.where(kpos <= qpos, s0, NEG)
    m0 = jnp.max(s0, axis=0, keepdims=True)
    p0 = jnp.exp(s0 - m0)
    l0 = jnp.sum(p0, axis=0, keepdims=True)
    acc0 = jnp.dot(vt_ref[own], p0.astype(BF16), preferred_element_type=F32)

    def body(it, carry):
        m, l, acc = carry
        scores = []
        for u in range(unroll):
            n = it * unroll + u
            start = pl.multiple_of(n * blk, blk)
            sn = _nt(k_ref[pl.ds(start, blk), :], q)
            scores.append(jnp.where(sel_ref[pl.ds(n, 1), :] > 0.0, sn, NEG))
        m_new = functools.reduce(jnp.maximum, [jnp.max(sn, axis=0, keepdims=True) for sn in scores], m)
        alpha = jnp.exp(m - m_new)
        l, acc = alpha * l, alpha * acc
        for u, sn in enumerate(scores):
            p = jnp.exp(sn - m_new)
            l = l + jnp.sum(p, axis=0, keepdims=True)
            acc = acc + jnp.dot(vt_ref[it * unroll + u], p.astype(BF16), preferred_element_type=F32)
        return m_new, l, acc

    trips = (own + unroll - 1) // unroll
    _, l, acc = lax.fori_loop(0, trips, body, (m0, l0, acc0))
    o_ref[...] = (acc / l).T.astype(o_ref.dtype)


def moba(proj, q_off, kr, vt, kmean, tables, unroll=4):
    s = proj.shape[0]
    nb = s // MOBA_BLOCK
    assert nb % unroll == 0
    tab = pl.BlockSpec((MOBA_BLOCK, HEAD_DIM), lambda h, i: (i, 0))
    return pl.pallas_call(
        functools.partial(_moba_kernel, unroll=unroll),
        grid=(N_HEADS, nb),
        in_specs=[pl.BlockSpec((MOBA_BLOCK, HEAD_DIM), lambda h, i: (i, q_off + h)),
                  tab, tab, tab,
                  pl.BlockSpec((s, HEAD_DIM), lambda h, i: (0, h)),
                  pl.BlockSpec((nb, HEAD_DIM, MOBA_BLOCK), lambda h, i: (0, h, 0)),
                  pl.BlockSpec((nb, HEAD_DIM), lambda h, i: (0, h))],
        out_specs=pl.BlockSpec((MOBA_BLOCK, HEAD_DIM), lambda h, i: (i, h)),
        out_shape=jax.ShapeDtypeStruct((s, HALF_WIDTH), BF16),
        scratch_shapes=[pltpu.VMEM((nb, MOBA_BLOCK), F32)],
        compiler_params=_params("arbitrary", "arbitrary"),
        name="moba",
    )(proj, *tables, kr, vt, kmean)


def _even_mixer(h, w_in, w_out, e, s5_params, glu_w):
    qkv = matmul(h, w_in, (e,), BF16, col0=0, ncols=3 * HALF_WIDTH)
    u = matmul(h, w_in, (e,), F32, col0=3 * HALF_WIDTH, ncols=HALF_WIDTH)
    o_a = stick_breaking(qkv, 0, N_HEADS, 2 * N_HEADS)
    o_b = gelu_glu(s5_scan(u, s5_params), glu_w, (e,))
    return matmul_acc([o_a, o_b], w_out, (e,))


def _odd_mixer(h, w_in, w_out, o, lb, norm_w):
    s = h.shape[0]
    proj = matmul(h, w_in, (o,), F32)
    o_c = hgrn2(proj, lb, norm_w, (0, N_HEADS, 2 * N_HEADS, 3 * N_HEADS))
    tables = rope_tables(s)
    kr, vt, kmean = moba_prep(proj, 5, 6, tables)
    o_d = moba(proj, 4 * N_HEADS, kr, vt, kmean, tables)
    return matmul_acc([o_c, o_d], w_out, (o,))


def kernel(x, c, ada_w, ada_table, norm_pre, norm_post, ffn_w1, ffn_w3, ffn_w2, ev_w_in, ev_w_out,
           s5_a_re, s5_a_im, s5_b_re, s5_b_im, s5_c_re, s5_c_im, s5_d, s5_log_dt, s5_glu_w,
           od_w_in, od_w_out, hgrn_lb, hgrn_norm_w):
    bsz, seq, d = x.shape
    depth = ada_table.shape[0]
    mod_shared = ada_project(c, ada_w).reshape(bsz, 9, d)
    lb_cum = jnp.cumsum(jax.nn.softmax(hgrn_lb.astype(F32), axis=0), axis=0)
    lb_all = lb_cum - lb_cum[:1]

    outs = []
    for b in range(bsz):
        xb = x[b]
        mods = [mod_shared[b] + ada_table[layer] for layer in range(depth)]
        subs = [(layer, slot) for layer in range(depth) for slot in range(3)]
        first = subs[0]
        h = pre_norm(xb, norm_pre[first[0], first[1]], mods[first[0]][1], mods[first[0]][0])
        for n, (layer, slot) in enumerate(subs):
            mod = mods[layer]
            if slot == 1:
                if layer % 2 == 0:
                    e = layer // 2
                    y = _even_mixer(h, ev_w_in, ev_w_out, e,
                                    (s5_a_re[e], s5_a_im[e], s5_b_re[e], s5_b_im[e], s5_c_re[e], s5_c_im[e],
                                     s5_d[e], s5_log_dt[e]), s5_glu_w)
                else:
                    o = layer // 2
                    y = _odd_mixer(h, od_w_in, od_w_out, o, lb_all[layer], hgrn_norm_w[o])
                res_w = 1.0
            else:
                f = slot // 2
                g = gate_up(h, ffn_w1, ffn_w3, (layer, f))
                y = matmul_acc([g], ffn_w2, (layer, f))
                res_w = 0.5
            nxt = None
            if n + 1 < len(subs):
                nl, ns = subs[n + 1]
                nxt = (norm_pre[nl, ns], mods[nl][3 * ns + 1], mods[nl][3 * ns])
            xb, h = post_norm(y, xb, norm_post[layer, slot], mod[3 * slot + 2], res_w, nxt)
        outs.append(xb)
    return jnp.stack(outs, axis=0)
```

```python
import functools
import math

import jax
import jax.numpy as jnp
from jax import lax
from jax.experimental import pallas as pl
from jax.experimental.pallas import tpu as pltpu

F32 = jnp.float32
BF16 = jnp.bfloat16

HEAD_DIM = 128
N_HEADS = 16
HALF_WIDTH = N_HEADS * HEAD_DIM
SSM_GROUP = 16
SSM_GROUPS = HALF_WIDTH // SSM_GROUP
SSM_STATE = 64
SSM_CHUNK = 16
SSM_PACK = HEAD_DIM // SSM_GROUP
HGRN_SUB = 16
SB_BLOCK = 256
MOBA_BLOCK = 256
MOBA_TOPK = 3
ROPE_THETA = 500000.0
ROPE_DIM = HEAD_DIM // 4
RMS_EPS = 1e-6
NEG = -1e30
EXP_UNDERFLOW = -104.0

VMEM_LIMIT = 56 * 1024 * 1024


def _params(*sem):
    return pltpu.CompilerParams(dimension_semantics=sem, vmem_limit_bytes=VMEM_LIMIT)


def _nt(a, b, **kw):
    return lax.dot_general(a, b, (((1,), (1,)), ((), ())), preferred_element_type=F32, **kw)


def _tn(a, b):
    return lax.dot_general(a, b, (((0,), (0,)), ((), ())), preferred_element_type=F32)


def _split_bf16(x, parts=2):
    out = []
    for _ in range(parts):
        p = x.astype(BF16)
        out.append(p)
        x = x - p.astype(F32)
    return out


def _wspec(lead, block, index_map):
    lead = tuple(lead)
    return pl.BlockSpec((None,) * len(lead) + tuple(block), lambda *g: lead + tuple(index_map(*g)))


def _ada_kernel(c_ref, w_ref, o_ref):
    c = c_ref[...]
    a = c * jax.nn.sigmoid(c)
    o_ref[...] = jnp.dot(a, w_ref[...], precision=lax.Precision.HIGHEST,
                         preferred_element_type=F32)


def ada_project(c, ada_w, tn=1024):
    d, n = ada_w.shape
    bsz = c.shape[0]
    assert bsz <= 8
    c8 = jnp.zeros((8, d), F32).at[:bsz].set(c.astype(F32))
    out = pl.pallas_call(
        _ada_kernel,
        grid=(n // tn,),
        in_specs=[pl.BlockSpec((8, d), lambda j: (0, 0)),
                  pl.BlockSpec((d, tn), lambda j: (0, j))],
        out_specs=pl.BlockSpec((8, tn), lambda j: (0, j)),
        out_shape=jax.ShapeDtypeStruct((8, n), F32),
        compiler_params=_params("arbitrary"),
        name="ada_project",
    )(c8, ada_w)
    return out[:bsz]


def _rms(x, gain):
    return x * lax.rsqrt(jnp.mean(x * x, axis=-1, keepdims=True) + RMS_EPS) * gain


def _pre_kernel(x_ref, vec_ref, h_ref):
    h = _rms(x_ref[...], vec_ref[0:1, :]) * (1.0 + vec_ref[1:2, :]) + vec_ref[2:3, :]
    h_ref[...] = h.astype(h_ref.dtype)


def pre_norm(x, g_pre, scale, shift, tm=256):
    m, d = x.shape
    vec = jnp.zeros((8, d), F32).at[0].set(g_pre).at[1].set(scale).at[2].set(shift)
    return pl.pallas_call(
        _pre_kernel,
        grid=(m // tm,),
        in_specs=[pl.BlockSpec((tm, d), lambda i: (i, 0)),
                  pl.BlockSpec((8, d), lambda i: (0, 0))],
        out_specs=pl.BlockSpec((tm, d), lambda i: (i, 0)),
        out_shape=jax.ShapeDtypeStruct((m, d), BF16),
        compiler_params=_params("arbitrary"),
        name="pre_norm",
    )(x, vec)


def _post_kernel(y_ref, x_ref, vec_ref, xo_ref, *maybe_h_ref, res_w):
    yn = _rms(y_ref[...].astype(F32), vec_ref[0:1, :])
    xn = x_ref[...] + (res_w * vec_ref[1:2, :]) * yn
    xo_ref[...] = xn
    if maybe_h_ref:
        h = _rms(xn, vec_ref[2:3, :]) * (1.0 + vec_ref[3:4, :]) + vec_ref[4:5, :]
        maybe_h_ref[0][...] = h.astype(BF16)


def post_norm(y, x, g_post, gate, res_w, nxt=None, tm=256):
    m, d = x.shape
    vec = jnp.zeros((8, d), F32).at[0].set(g_post).at[1].set(gate)
    out_shape = [jax.ShapeDtypeStruct((m, d), F32)]
    out_specs = [pl.BlockSpec((tm, d), lambda i: (i, 0))]
    if nxt is not None:
        vec = vec.at[2].set(nxt[0]).at[3].set(nxt[1]).at[4].set(nxt[2])
        out_shape.append(jax.ShapeDtypeStruct((m, d), BF16))
        out_specs.append(pl.BlockSpec((tm, d), lambda i: (i, 0)))
    res = pl.pallas_call(
        functools.partial(_post_kernel, res_w=res_w),
        grid=(m // tm,),
        in_specs=[pl.BlockSpec((tm, d), lambda i: (i, 0)),
                  pl.BlockSpec((tm, d), lambda i: (i, 0)),
                  pl.BlockSpec((8, d), lambda i: (0, 0))],
        out_specs=out_specs,
        out_shape=out_shape,
        compiler_params=_params("arbitrary"),
        name="post_norm",
    )(y, x, vec)
    return (res[0], res[1]) if nxt is not None else (res[0], None)


def _mm_kernel(a_ref, w_ref, o_ref):
    o_ref[...] = jnp.dot(a_ref[...], w_ref[...].astype(BF16),
                         preferred_element_type=F32).astype(o_ref.dtype)


def matmul(a, w, lead, out_dtype, col0=0, ncols=None, tm=1024, tn=512):
    m, k = a.shape
    n = w.shape[-1] - col0 if ncols is None else ncols
    tm, tn = min(tm, m), min(tn, n)
    assert col0 % tn == 0 and n % tn == 0 and m % tm == 0
    j0 = col0 // tn
    return pl.pallas_call(
        _mm_kernel,
        grid=(m // tm, n // tn),
        in_specs=[pl.BlockSpec((tm, k), lambda i, j: (i, 0)),
                  _wspec(lead, (k, tn), lambda i, j: (0, j + j0))],
        out_specs=pl.BlockSpec((tm, tn), lambda i, j: (i, j)),
        out_shape=jax.ShapeDtypeStruct((m, n), out_dtype),
        compiler_params=_params("arbitrary", "arbitrary"),
        name="matmul",
    )(a, w)


def _gateup_kernel(a_ref, w1_ref, w3_ref, o_ref, *, tc):
    a = a_ref[...]
    for n0 in range(0, o_ref.shape[1], tc):
        g = jnp.dot(a, w1_ref[:, n0:n0 + tc].astype(BF16), preferred_element_type=F32)
        u = jnp.dot(a, w3_ref[:, n0:n0 + tc].astype(BF16), preferred_element_type=F32)
        o_ref[:, n0:n0 + tc] = (g * jax.nn.sigmoid(g) * u).astype(o_ref.dtype)


def gate_up(a, w1, w3, lead, tm=1024, tn=512):
    m, k = a.shape
    n = w1.shape[-1]
    tm, tn = min(tm, m), min(tn, n)
    wspec = _wspec(lead, (k, tn), lambda i, j: (0, j))
    return pl.pallas_call(
        functools.partial(_gateup_kernel, tc=min(256, tn)),
        grid=(m // tm, pl.cdiv(n, tn)),
        in_specs=[pl.BlockSpec((tm, k), lambda i, j: (i, 0), pipeline_mode=pl.Buffered(1)), wspec, wspec],
        out_specs=pl.BlockSpec((tm, tn), lambda i, j: (i, j)),
        out_shape=jax.ShapeDtypeStruct((m, n), BF16),
        compiler_params=_params("arbitrary", "arbitrary"),
        name="gate_up",
    )(a, w1, w3)


def _mm_acc_kernel(*refs, nks, tails, tn):
    a_refs, w_ref, o_ref = refs[:len(nks)], refs[len(nks)], refs[len(nks) + 1]
    kk = pl.program_id(1)
    tk = w_ref.shape[0]

    @pl.when(kk == 0)
    def _():
        o_ref[...] = jnp.zeros_like(o_ref)

    def accumulate(a, valid):
        if valid < tk:
            a = jnp.where(lax.broadcasted_iota(jnp.int32, a.shape, 1) < valid, a, jnp.zeros_like(a))
        for n0 in range(0, o_ref.shape[1], tn):
            w = w_ref[:, n0:n0 + tn]
            if valid < tk:
                w = jnp.where(lax.broadcasted_iota(jnp.int32, w.shape, 0) < valid, w, 0.0)
            o_ref[:, n0:n0 + tn] += jnp.dot(a, w.astype(BF16), preferred_element_type=F32)

    start = 0
    for a_ref, nk, tail in zip(a_refs, nks, tails):
        local = kk - start
        if tail == tk:
            @pl.when(jnp.logical_and(local >= 0, local < nk))
            def _(a_ref=a_ref):
                accumulate(a_ref[...], tk)
        else:
            @pl.when(jnp.logical_and(local >= 0, local < nk - 1))
            def _(a_ref=a_ref):
                accumulate(a_ref[...], tk)

            @pl.when(local == nk - 1)
            def _(a_ref=a_ref, tail=tail):
                accumulate(a_ref[...], tail)
        start += nk


def matmul_acc(a_list, w, lead, tm=1024, tk=512):
    m = a_list[0].shape[0]
    n = w.shape[-1]
    tm = min(tm, m)
    assert all(a.shape[1] % tk == 0 for a in a_list[:-1])
    nks = tuple(pl.cdiv(a.shape[1], tk) for a in a_list)
    tails = tuple(a.shape[1] - (nk - 1) * tk for a, nk in zip(a_list, nks))
    starts = [sum(nks[:p]) for p in range(len(nks))]
    a_specs = [pl.BlockSpec((tm, tk), functools.partial(
        lambda i, kk, s0, nk: (i, jnp.clip(kk - s0, 0, nk - 1)), s0=s0, nk=nk))
        for s0, nk in zip(starts, nks)]
    return pl.pallas_call(
        functools.partial(_mm_acc_kernel, nks=nks, tails=tails, tn=min(512, n)),
        grid=(m // tm, sum(nks)),
        in_specs=a_specs + [_wspec(lead, (tk, n), lambda i, kk: (kk, 0))],
        out_specs=pl.BlockSpec((tm, n), lambda i, kk: (i, 0)),
        out_shape=jax.ShapeDtypeStruct((m, n), F32),
        compiler_params=_params("arbitrary", "arbitrary"),
        name="matmul_acc",
    )(*a_list, w)


def _sb_kernel(q_ref, k_ref, v_ref, o_ref, *, t):
    i = pl.program_id(1)
    nch = q_ref.shape[0] // t
    qs = [q_ref[ch * t:(ch + 1) * t, :] for ch in range(nch)]
    row = lax.broadcasted_iota(jnp.int32, (t, t), 0)
    col = lax.broadcasted_iota(jnp.int32, (t, t), 1)
    suffix = (row > col).astype(BF16)
    scale = HEAD_DIM ** -0.5

    def cond(carry):
        step, cs, _ = carry
        top = functools.reduce(jnp.maximum, [jnp.max(c) for c in cs])
        return jnp.logical_and(nch * i + nch - 1 - step >= 0, top >= EXP_UNDERFLOW)

    def body(carry):
        step, cs, accs = carry
        new_c, new_acc = [], []
        for ch in range(nch):
            qblk = nch * i + ch
            j = qblk - step
            jc = jnp.maximum(j, 0)
            start = pl.multiple_of(jc * t, t)
            kj = k_ref[pl.ds(start, t), :]
            vj = v_ref[pl.ds(start, t), :]
            z = _nt(qs[ch], kj) * scale
            mask = (col - row) < jnp.where(j >= 0, (qblk - jc) * t, -2 * t)
            ls_pos = jnp.minimum(z, 0.0) - jnp.log(1.0 + jnp.exp(-jnp.abs(z)))
            lk = jnp.where(mask, ls_pos - z, 0.0)
            later = sum(jnp.dot(p, suffix, preferred_element_type=F32) for p in _split_bf16(lk))
            w = jnp.where(mask, jnp.exp(ls_pos + later + cs[ch]), 0.0)
            new_acc.append(accs[ch] + jnp.dot(w.astype(BF16), vj, preferred_element_type=F32))
            new_c.append(cs[ch] + jnp.sum(lk, axis=1, keepdims=True))
        return step + 1, tuple(new_c), tuple(new_acc)

    init = (jnp.int32(0), tuple(jnp.zeros((t, 1), F32) for _ in range(nch)),
            tuple(jnp.zeros((t, HEAD_DIM), F32) for _ in range(nch)))
    _, _, accs = lax.while_loop(cond, body, init)
    for ch in range(nch):
        o_ref[ch * t:(ch + 1) * t, :] = accs[ch].astype(o_ref.dtype)


def stick_breaking(proj, q_off, k_off, v_off, chains=4):
    s = proj.shape[0]
    t = min(SB_BLOCK, s)
    tq = min(chains * t, s)
    return pl.pallas_call(
        functools.partial(_sb_kernel, t=t),
        grid=(N_HEADS, s // tq),
        in_specs=[pl.BlockSpec((tq, HEAD_DIM), lambda h, i: (i, q_off + h)),
                  pl.BlockSpec((s, HEAD_DIM), lambda h, i: (0, k_off + h)),
                  pl.BlockSpec((s, HEAD_DIM), lambda h, i: (0, v_off + h))],
        out_specs=pl.BlockSpec((tq, HEAD_DIM), lambda h, i: (i, h)),
        out_shape=jax.ShapeDtypeStruct((s, HALF_WIDTH), BF16),
        compiler_params=_params("arbitrary", "arbitrary"),
        name="stick_breaking",
    )(proj, proj, proj)


def _s5_spread(rows, cols, row_period, col_period, row_div, col_div):
    r = lax.broadcasted_iota(jnp.int32, (rows, cols), 0)
    c = lax.broadcasted_iota(jnp.int32, (rows, cols), 1)
    same = jnp.logical_and(r // row_div == c // col_div, r % row_period == c % col_period)
    return same.astype(BF16)


def _s5_expand(compact, spread, rows_per_group, lanes_per_group):
    full = jnp.dot(compact, spread, preferred_element_type=F32)
    r = lax.broadcasted_iota(jnp.int32, full.shape, 0)
    c = lax.broadcasted_iota(jnp.int32, full.shape, 1)
    keep = (r // rows_per_group) % SSM_PACK == (c // lanes_per_group) % SSM_PACK
    return jnp.where(keep, full, 0.0).astype(BF16)


def _s5_kernel(u_ref, toep_ref, win_ref, vout_ref, ar_ref, ai_ref, y_ref, sin_ref, acc_ref):
    L, H, P = SSM_CHUNK, SSM_GROUP, SSM_STATE
    nc = u_ref.shape[0] // L
    jc = pl.program_id(1)
    per = acc_ref.shape[1] // HEAD_DIM
    xp = [jnp.concatenate([u_ref[pl.ds(2 * p, nc, stride=L), :], u_ref[pl.ds(2 * p + 1, nc, stride=L), :]],
                          axis=1).astype(BF16) for p in range(L // 2)]
    pair = lambda ref, p: ref[0, 0, 2 * p:2 * p + 2].reshape(2 * HEAD_DIM, ref.shape[-1])

    @pl.when(jc == 0)
    def _():
        nstate = sin_ref.shape[1]
        half = nstate // 2
        spread = _s5_spread(2 * P, nstate, P, P, P, half)
        b = sum(jnp.dot(xp[p], _s5_expand(pair(win_ref, p), spread, H, P), preferred_element_type=F32)
                for p in range(L // 2))
        er, ei = b[:, :half], b[:, half:]
        row = lax.broadcasted_iota(jnp.int32, er.shape, 0)
        for k in range(int(math.log2(nc))):
            sh = 1 << k
            sr = jnp.where(row >= sh, pltpu.roll(er, sh, axis=0), 0.0)
            si = jnp.where(row >= sh, pltpu.roll(ei, sh, axis=0), 0.0)
            ar, ai = ar_ref[0, k:k + 1, :], ai_ref[0, k:k + 1, :]
            er, ei = er + ar * sr - ai * si, ei + ar * si + ai * sr
        sin_ref[:, :half] = jnp.where(row >= 1, pltpu.roll(er, 1, axis=0), 0.0).astype(BF16)
        sin_ref[:, half:] = jnp.where(row >= 1, pltpu.roll(ei, 1, axis=0), 0.0).astype(BF16)

    spread = _s5_spread(per * H, per * HEAD_DIM, H, H, H, HEAD_DIM)
    acc_ref[...] = jnp.dot(sin_ref[...], _s5_expand(vout_ref[0, 0], spread, P, H),
                           preferred_element_type=F32)
    for p in range(L // 2):
        @pl.when(2 * p < (jc + 1) * per)
        def _(p=p):
            acc_ref[...] += jnp.dot(xp[p], _s5_expand(pair(toep_ref, p), spread, H, H),
                                    preferred_element_type=F32)
    for t in range(per):
        y_ref[pl.ds(jc * per + t, nc, stride=L), :] = acc_ref[:, t * HEAD_DIM:(t + 1) * HEAD_DIM]


def s5_tables(a_re, a_im, b_re, b_im, c_re, c_im, d_skip, log_dt, ncol):
    L, P, H, G, K8 = SSM_CHUNK, SSM_STATE, SSM_GROUP, SSM_GROUPS, SSM_PACK
    GC = G // K8
    per = L // ncol
    dt = jnp.exp(log_dt.astype(F32))[:, None]
    ar, ai = a_re.astype(F32), a_im.astype(F32)
    mag = jnp.exp(ar * dt)
    lr, li = mag * jnp.cos(ai * dt), mag * jnp.sin(ai * dt)
    den = ar * ar + ai * ai
    nr, ni = lr - 1.0, li
    cr = (nr * ar + ni * ai) / den
    ci = (ni * ar - nr * ai) / den
    br, bi = b_re.astype(F32), b_im.astype(F32)
    bbr = cr[..., None] * br - ci[..., None] * bi
    bbi = cr[..., None] * bi + ci[..., None] * br
    pr, pi = [jnp.ones_like(lr)], [jnp.zeros_like(li)]
    for _ in range(L):
        pr, pi = pr + [pr[-1] * lr - pi[-1] * li], pi + [pr[-1] * li + pi[-1] * lr]
    pr, pi = jnp.stack(pr, 1), jnp.stack(pi, 1)
    ccr, cci = c_re.astype(F32), c_im.astype(F32)
    clr = ccr[:, None] * pr[:, :L, None, :] - cci[:, None] * pi[:, :L, None, :]
    cli = ccr[:, None] * pi[:, :L, None, :] + cci[:, None] * pr[:, :L, None, :]
    kern = (jnp.einsum('gdnp,gph->gdnh', clr, bbr, precision='highest')
            - jnp.einsum('gdnp,gph->gdnh', cli, bbi, precision='highest'))
    kern = kern.at[:, 0].add(jnp.eye(H, dtype=F32)[None] * d_skip.astype(F32).reshape(G, H)[:, :, None])
    tau = jnp.arange(L)
    diff = tau[None, :] - tau[:, None]
    kt = kern[:, jnp.clip(diff, 0, L - 1)]
    kt = jnp.where((diff >= 0)[None, :, :, None, None], kt, 0.0)
    toep = kt.reshape(GC, K8, L, ncol, per, H, H).transpose(0, 3, 2, 1, 6, 4, 5)
    toep = toep.reshape(GC, ncol, L, K8 * H, per * H)
    qr, qi = pr[:, L - 1 - tau], pi[:, L - 1 - tau]
    bt_r, bt_i = bbr.transpose(0, 2, 1)[:, None], bbi.transpose(0, 2, 1)[:, None]
    wre = qr[:, :, None, :] * bt_r - qi[:, :, None, :] * bt_i
    wim = qr[:, :, None, :] * bt_i + qi[:, :, None, :] * bt_r
    win = jnp.concatenate([wre, wim], -1)
    win = win.reshape(GC, K8, L, H, 2 * P).transpose(0, 2, 1, 3, 4).reshape(GC, 1, L, K8 * H, 2 * P)
    pr1, pi1 = pr[:, 1:], pi[:, 1:]
    vre = ccr[:, None] * pr1[:, :, None, :] - cci[:, None] * pi1[:, :, None, :]
    vim = ccr[:, None] * pi1[:, :, None, :] + cci[:, None] * pr1[:, :, None, :]
    vout = jnp.stack([vre, -vim], 1)
    vout = vout.reshape(GC, K8, 2, ncol, per, H, P).transpose(0, 3, 2, 1, 6, 4, 5)
    vout = vout.reshape(GC, ncol, 2 * K8 * P, per * H)
    zr, zi = [pr[:, L]], [pi[:, L]]
    for _ in range(15):
        zr, zi = zr + [zr[-1] * zr[-1] - zi[-1] * zi[-1]], zi + [2.0 * zr[-1] * zi[-1]]
    pack = lambda z: jnp.stack(z, 1).reshape(GC, K8, 16, P).transpose(0, 2, 1, 3).reshape(GC, 16, K8 * P)
    return toep.astype(BF16), win.astype(BF16), vout.astype(BF16), pack(zr), pack(zi)


def s5_scan(u, s5_params, ncol=4):
    toep, win, vout, zr, zi = s5_tables(*s5_params, ncol=ncol)
    s, width = u.shape
    L, K8, P = SSM_CHUNK, SSM_PACK, SSM_STATE
    gc = width // HEAD_DIM
    cw = (L // ncol) * HEAD_DIM
    nstate = 2 * K8 * P
    return pl.pallas_call(
        _s5_kernel,
        grid=(gc, ncol),
        in_specs=[pl.BlockSpec((s, HEAD_DIM), lambda g, j: (0, g)),
                  pl.BlockSpec((1, 1) + toep.shape[2:], lambda g, j: (g, j, 0, 0, 0)),
                  pl.BlockSpec((1, 1) + win.shape[2:], lambda g, j: (g, 0, 0, 0, 0)),
                  pl.BlockSpec((1, 1) + vout.shape[2:], lambda g, j: (g, j, 0, 0)),
                  pl.BlockSpec((1, 16, nstate // 2), lambda g, j: (g, 0, 0)),
                  pl.BlockSpec((1, 16, nstate // 2), lambda g, j: (g, 0, 0))],
        out_specs=pl.BlockSpec((s, HEAD_DIM), lambda g, j: (0, g)),
        out_shape=jax.ShapeDtypeStruct((s, width), F32),
        scratch_shapes=[pltpu.VMEM((s // L, nstate), BF16), pltpu.VMEM((s // L, cw), F32)],
        compiler_params=_params("arbitrary", "arbitrary"),
        name="s5_scan",
    )(u, toep, win, vout, zr, zi)


def _glu_kernel(y_ref, yj_ref, w_ref, o_ref):
    yg = jax.nn.gelu(y_ref[...])
    z = jnp.dot(yg.astype(BF16), w_ref[...].astype(BF16), preferred_element_type=F32)
    o_ref[...] = (jax.nn.gelu(yj_ref[...]) * jax.nn.sigmoid(z)).astype(o_ref.dtype)


def gelu_glu(y, glu_w, lead, tm=512, tn=512):
    m, k = y.shape
    tm = min(tm, m)
    return pl.pallas_call(
        _glu_kernel,
        grid=(m // tm, k // tn),
        in_specs=[pl.BlockSpec((tm, k), lambda i, j: (i, 0)),
                  pl.BlockSpec((tm, tn), lambda i, j: (i, j)),
                  _wspec(lead, (k, tn), lambda i, j: (0, j))],
        out_specs=pl.BlockSpec((tm, tn), lambda i, j: (i, j)),
        out_shape=jax.ShapeDtypeStruct((m, k), BF16),
        compiler_params=_params("arbitrary", "arbitrary"),
        name="gelu_glu",
    )(y, y, glu_w)


def _hgrn_kernel(q_ref, f_ref, i_ref, g_ref, lb_ref, nw_ref, o_ref, state_ref):
    @pl.when(pl.program_id(1) == 0)
    def _():
        state_ref[...] = jnp.zeros_like(state_ref)

    tc, d = q_ref.shape
    sub = HGRN_SUB
    nblk = tc // sub
    lb = lb_ref[...]
    q = q_ref[...]
    v = i_ref[...]
    f = lb + (1.0 - lb) * jax.nn.sigmoid(f_ref[...])
    k = 1.0 - f
    r_i = lax.broadcasted_iota(jnp.int32, (tc, tc), 0)
    c_i = lax.broadcasted_iota(jnp.int32, (tc, tc), 1)
    tril = (c_i <= r_i).astype(BF16)
    gcum = sum(jnp.dot(tril, p, preferred_element_type=F32)
               for p in _split_bf16(jnp.log(f), parts=3))
    gtot = gcum[tc - 1:tc, :]
    state = state_ref[...]
    o = _nt((q * jnp.exp(gcum)).astype(BF16), state.astype(BF16))

    g3, q3, k3, v3 = (x.reshape(nblk, sub, d) for x in (gcum, q, k, v))
    s_idx = lax.broadcasted_iota(jnp.int32, (nblk, sub, d), 1)
    rows = []
    for t in range(sub):
        dec = jnp.exp(jnp.where(s_idx <= t, g3[:, t:t + 1] - g3, -jnp.inf))
        a = jnp.sum(q3[:, t:t + 1] * k3 * dec, axis=2, keepdims=True)
        rows.append(jnp.sum(a * v3, axis=1, keepdims=True))
    o = o + jnp.concatenate(rows, axis=1).reshape(tc, d)

    v16 = v.astype(BF16)
    b = sub
    while 2 * b <= tc:
        m = tc // (2 * b)
        g4, q4, k4 = (x.reshape(m, 2 * b, d) for x in (gcum, q, k))
        ref = g4[:, b:b + 1]
        qd = (q4[:, b:] * jnp.exp(g4[:, b:] - ref)).astype(BF16)
        kd = (k4[:, :b] * jnp.exp(ref - g4[:, :b])).astype(BF16)
        att = jnp.einsum('mqd,mkd->mqk', qd, kd, preferred_element_type=F32)
        ob = jnp.einsum('mqk,mkd->mqd', att.astype(BF16), v16.reshape(m, 2 * b, d)[:, :b],
                        preferred_element_type=F32)
        o = o + jnp.concatenate([jnp.zeros_like(ob), ob], axis=1).reshape(tc, d)
        b *= 2

    kdec = (k * jnp.exp(gtot - gcum)).astype(BF16)
    state_ref[...] = state * jnp.exp(gtot) + _tn(v16, kdec)
    on = o * lax.rsqrt(jnp.mean(o * o, axis=-1, keepdims=True) + RMS_EPS) * nw_ref[...]
    gate = g_ref[...]
    o_ref[...] = (on * (gate * jax.nn.sigmoid(gate))).astype(o_ref.dtype)


def hgrn2(proj, lb, norm_w, offs, tc=512):
    s = proj.shape[0]
    tc = min(tc, s)
    specs = [pl.BlockSpec((tc, HEAD_DIM), functools.partial(lambda h, i, o: (i, o + h), o=o)) for o in offs]
    return pl.pallas_call(
        _hgrn_kernel,
        grid=(N_HEADS, s // tc),
        in_specs=specs + [pl.BlockSpec((1, HEAD_DIM), lambda h, i: (0, h)),
                          pl.BlockSpec((1, HEAD_DIM), lambda h, i: (0, 0))],
        out_specs=pl.BlockSpec((tc, HEAD_DIM), lambda h, i: (i, h)),
        out_shape=jax.ShapeDtypeStruct((s, HALF_WIDTH), BF16),
        scratch_shapes=[pltpu.VMEM((HEAD_DIM, HEAD_DIM), F32)],
        compiler_params=_params("arbitrary", "arbitrary"),
        name="hgrn2",
    )(proj, proj, proj, proj, lb.reshape(1, HALF_WIDTH).astype(F32), norm_w.reshape(1, HEAD_DIM).astype(F32))


def rope_tables(s):
    half = ROPE_DIM // 2
    inv = ROPE_THETA ** (-jnp.arange(half, dtype=F32) / half)
    ang = jnp.arange(s).astype(F32)[:, None] * inv[None, :]
    cos, sin = jnp.cos(ang), jnp.sin(ang)
    one = jnp.ones((s, HEAD_DIM - ROPE_DIM), F32)
    zero = jnp.zeros((s, HEAD_DIM - ROPE_DIM), F32)
    z16 = jnp.zeros((s, half), F32)
    c = jnp.concatenate([cos, cos, one], 1)
    s_up = jnp.concatenate([z16, sin, zero], 1)
    s_dn = jnp.concatenate([-sin, z16, zero], 1)
    return c, s_up, s_dn


def _rope(x, c, s_up, s_dn):
    half = ROPE_DIM // 2
    return (x * c + pltpu.roll(x, half, axis=1) * s_up
            + pltpu.roll(x, HEAD_DIM - half, axis=1) * s_dn)


def _moba_prep_kernel(k_ref, v_ref, c_ref, su_ref, sd_ref, kr_ref, vt_ref, km_ref):
    c, su, sd = c_ref[...], su_ref[...], sd_ref[...]
    for h in range(N_HEADS):
        sl = slice(h * HEAD_DIM, (h + 1) * HEAD_DIM)
        kr = _rope(k_ref[:, sl], c, su, sd)
        kr_ref[:, sl] = kr.astype(BF16)
        km_ref[0, :, sl] = jnp.mean(kr, axis=0, keepdims=True)
    vt_ref[0] = v_ref[...].T.astype(BF16)


def moba_prep(proj, k_blk, v_blk, tables):
    s = proj.shape[0]
    nb = s // MOBA_BLOCK
    tab = pl.BlockSpec((MOBA_BLOCK, HEAD_DIM), lambda n: (n, 0))
    kr, vt, km = pl.pallas_call(
        _moba_prep_kernel,
        grid=(nb,),
        in_specs=[pl.BlockSpec((MOBA_BLOCK, HALF_WIDTH), lambda n: (n, k_blk)),
                  pl.BlockSpec((MOBA_BLOCK, HALF_WIDTH), lambda n: (n, v_blk)),
                  tab, tab, tab],
        out_specs=[pl.BlockSpec((MOBA_BLOCK, HALF_WIDTH), lambda n: (n, 0)),
                   pl.BlockSpec((1, HALF_WIDTH, MOBA_BLOCK), lambda n: (n, 0, 0)),
                   pl.BlockSpec((1, 1, HALF_WIDTH), lambda n: (n, 0, 0))],
        out_shape=[jax.ShapeDtypeStruct((s, HALF_WIDTH), BF16),
                   jax.ShapeDtypeStruct((nb, HALF_WIDTH, MOBA_BLOCK), BF16),
                   jax.ShapeDtypeStruct((nb, 1, HALF_WIDTH), F32)],
        compiler_params=_params("arbitrary"),
        name="moba_prep",
    )(proj, proj, *tables)
    return kr, vt, km.reshape(nb, HALF_WIDTH)


def _moba_kernel(q_ref, c_ref, su_ref, sd_ref, k_ref, vt_ref, km_ref, o_ref, sel_ref, *, unroll):
    own = pl.program_id(1)
    blk, d = MOBA_BLOCK, HEAD_DIM
    nb = km_ref.shape[0]
    heads = q_ref.shape[1] // d
    cols = lambda hh: slice(hh * d, (hh + 1) * d)
    c, su, sd = c_ref[...], su_ref[...], sd_ref[...]
    o0 = pl.multiple_of(own * blk, blk)

    qs, init = [], []
    for hh in range(heads):
        qf = _rope(q_ref[:, cols(hh)], c, su, sd)
        q = (qf * d ** -0.5).astype(BF16)
        qs.append(q)
        gate = _nt(km_ref[:, cols(hh)], qf, precision=lax.Precision.HIGHEST)
        row = lax.broadcasted_iota(jnp.int32, gate.shape, 0)
        g = jnp.where(row < own, gate, NEG)
        sel = jnp.zeros(gate.shape, F32)
        for j in range(MOBA_TOPK):
            m = jnp.max(g, axis=0, keepdims=True)
            idx = jnp.min(jnp.where(g == m, row, nb), axis=0, keepdims=True)
            pick = row == idx
            sel = jnp.where(pick, jnp.where(j < own, 1.0, 0.0), sel)
            g = jnp.where(pick, -jnp.inf, g)
        sel_ref[hh] = sel
        s0 = _nt(k_ref[pl.ds(o0, blk), cols(hh)], q)
        kpos = lax.broadcasted_iota(jnp.int32, s0.shape, 0)
        qpos = lax.broadcasted_iota(jnp.int32, s0.shape, 1)
        s0 = jnp.where(kpos <= qpos, s0, NEG)
        m0 = jnp.max(s0, axis=0, keepdims=True)
        p0 = jnp.exp(s0 - m0)
        l0 = jnp.sum(p0, axis=0, keepdims=True)
        acc0 = jnp.dot(vt_ref[own, cols(hh), :], p0.astype(BF16), preferred_element_type=F32)
        init.append((m0, l0, acc0))

    def body(it, carry):
        out = []
        for hh in range(heads):
            m, l, acc = carry[hh]
            scores = []
            for u in range(unroll):
                n = it * unroll + u
                start = pl.multiple_of(n * blk, blk)
                sn = _nt(k_ref[pl.ds(start, blk), cols(hh)], qs[hh])
                scores.append(jnp.where(sel_ref[hh, pl.ds(n, 1), :] > 0.0, sn, NEG))
            m_new = functools.reduce(jnp.maximum, [jnp.max(sn, axis=0, keepdims=True) for sn in scores], m)
            alpha = jnp.exp(m - m_new)
            l, acc = alpha * l, alpha * acc
            for u, sn in enumerate(scores):
                p = jnp.exp(sn - m_new)
                l = l + jnp.sum(p, axis=0, keepdims=True)
                acc = acc + jnp.dot(vt_ref[it * unroll + u, cols(hh), :], p.astype(BF16),
                                    preferred_element_type=F32)
            out.append((m_new, l, acc))
        return tuple(out)

    trips = (own + unroll - 1) // unroll
    final = lax.fori_loop(0, trips, body, tuple(init))
    for hh in range(heads):
        _, l, acc = final[hh]
        o_ref[:, cols(hh)] = (acc / l).T.astype(o_ref.dtype)


def moba(proj, q_off, kr, vt, kmean, tables, unroll=4, heads=2):
    s = proj.shape[0]
    nb = s // MOBA_BLOCK
    assert nb % unroll == 0 and q_off % heads == 0 and N_HEADS % heads == 0
    w = heads * HEAD_DIM
    tab = pl.BlockSpec((MOBA_BLOCK, HEAD_DIM), lambda h, i: (i, 0))
    return pl.pallas_call(
        functools.partial(_moba_kernel, unroll=unroll),
        grid=(N_HEADS // heads, nb),
        in_specs=[pl.BlockSpec((MOBA_BLOCK, w), lambda h, i: (i, q_off // heads + h)),
                  tab, tab, tab,
                  pl.BlockSpec((s, w), lambda h, i: (0, h)),
                  pl.BlockSpec((nb, w, MOBA_BLOCK), lambda h, i: (0, h, 0)),
                  pl.BlockSpec((nb, w), lambda h, i: (0, h))],
        out_specs=pl.BlockSpec((MOBA_BLOCK, w), lambda h, i: (i, h)),
        out_shape=jax.ShapeDtypeStruct((s, HALF_WIDTH), BF16),
        scratch_shapes=[pltpu.VMEM((heads, nb, MOBA_BLOCK), F32)],
        compiler_params=_params("arbitrary", "arbitrary"),
        name="moba",
    )(proj, *tables, kr, vt, kmean)


def _even_mixer(h, w_in, w_out, e, s5_params, glu_w):
    qkv = matmul(h, w_in, (e,), BF16, col0=0, ncols=3 * HALF_WIDTH)
    u = matmul(h, w_in, (e,), F32, col0=3 * HALF_WIDTH, ncols=HALF_WIDTH)
    o_a = stick_breaking(qkv, 0, N_HEADS, 2 * N_HEADS)
    o_b = gelu_glu(s5_scan(u, s5_params), glu_w, (e,))
    return matmul_acc([o_a, o_b], w_out, (e,))


def _odd_mixer(h, w_in, w_out, o, lb, norm_w):
    s = h.shape[0]
    proj = matmul(h, w_in, (o,), F32)
    o_c = hgrn2(proj, lb, norm_w, (0, N_HEADS, 2 * N_HEADS, 3 * N_HEADS))
    tables = rope_tables(s)
    kr, vt, kmean = moba_prep(proj, 5, 6, tables)
    o_d = moba(proj, 4 * N_HEADS, kr, vt, kmean, tables)
    return matmul_acc([o_c, o_d], w_out, (o,))


def kernel(x, c, ada_w, ada_table, norm_pre, norm_post, ffn_w1, ffn_w3, ffn_w2, ev_w_in, ev_w_out,
           s5_a_re, s5_a_im, s5_b_re, s5_b_im, s5_c_re, s5_c_im, s5_d, s5_log_dt, s5_glu_w,
           od_w_in, od_w_out, hgrn_lb, hgrn_norm_w):
    bsz, seq, d = x.shape
    depth = ada_table.shape[0]
    mod_shared = ada_project(c, ada_w).reshape(bsz, 9, d)
    lb_cum = jnp.cumsum(jax.nn.softmax(hgrn_lb.astype(F32), axis=0), axis=0)
    lb_all = lb_cum - lb_cum[:1]

    outs = []
    for b in range(bsz):
        xb = x[b]
        mods = [mod_shared[b] + ada_table[layer] for layer in range(depth)]
        subs = [(layer, slot) for layer in range(depth) for slot in range(3)]
        first = subs[0]
        h = pre_norm(xb, norm_pre[first[0], first[1]], mods[first[0]][1], mods[first[0]][0])
        for n, (layer, slot) in enumerate(subs):
            mod = mods[layer]
            if slot == 1:
                if layer % 2 == 0:
                    e = layer // 2
                    y = _even_mixer(h, ev_w_in, ev_w_out, e,
                                    (s5_a_re[e], s5_a_im[e], s5_b_re[e], s5_b_im[e], s5_c_re[e], s5_c_im[e],
                                     s5_d[e], s5_log_dt[e]), s5_glu_w)
                else:
                    o = layer // 2
                    y = _odd_mixer(h, od_w_in, od_w_out, o, lb_all[layer], hgrn_norm_w[o])
                res_w = 1.0
            else:
                f = slot // 2
                g = gate_up(h, ffn_w1, ffn_w3, (layer, f))
                y = matmul_acc([g], ffn_w2, (layer, f))
                res_w = 0.5
            nxt = None
            if n + 1 < len(subs):
                nl, ns = subs[n + 1]
                nxt = (norm_pre[nl, ns], mods[nl][3 * ns + 1], mods[nl][3 * ns])
            xb, h = post_norm(y, xb, norm_post[layer, slot], mod[3 * slot + 2], res_w, nxt)
        outs.append(xb)
    return jnp.stack(outs, axis=0)
```

```python
import functools
import math

import jax
import jax.numpy as jnp
from jax import lax
from jax.experimental import pallas as pl
from jax.experimental.pallas import tpu as pltpu

F32 = jnp.float32
BF16 = jnp.bfloat16

HEAD_DIM = 128
N_HEADS = 16
HALF_WIDTH = N_HEADS * HEAD_DIM
SSM_GROUP = 16
SSM_GROUPS = HALF_WIDTH // SSM_GROUP
SSM_STATE = 64
SSM_CHUNK = 16
SSM_PACK = HEAD_DIM // SSM_GROUP
HGRN_SUB = 16
SB_BLOCK = 256
MOBA_BLOCK = 256
MOBA_TOPK = 3
ROPE_THETA = 500000.0
ROPE_DIM = HEAD_DIM // 4
RMS_EPS = 1e-6
NEG = -1e30
EXP_UNDERFLOW = -104.0

VMEM_LIMIT = 56 * 1024 * 1024


def _params(*sem):
    return pltpu.CompilerParams(dimension_semantics=sem, vmem_limit_bytes=VMEM_LIMIT)


def _nt(a, b, **kw):
    return lax.dot_general(a, b, (((1,), (1,)), ((), ())), preferred_element_type=F32, **kw)


def _tn(a, b):
    return lax.dot_general(a, b, (((0,), (0,)), ((), ())), preferred_element_type=F32)


def _split_bf16(x, parts=2):
    out = []
    for _ in range(parts):
        p = x.astype(BF16)
        out.append(p)
        x = x - p.astype(F32)
    return out


def _wspec(lead, block, index_map):
    lead = tuple(lead)
    return pl.BlockSpec((None,) * len(lead) + tuple(block), lambda *g: lead + tuple(index_map(*g)))


def _ada_kernel(c_ref, w_ref, o_ref):
    c = c_ref[...]
    a = c * jax.nn.sigmoid(c)
    o_ref[...] = jnp.dot(a, w_ref[...], precision=lax.Precision.HIGHEST,
                         preferred_element_type=F32)


def ada_project(c, ada_w, tn=1024):
    d, n = ada_w.shape
    bsz = c.shape[0]
    assert bsz <= 8
    c8 = jnp.zeros((8, d), F32).at[:bsz].set(c.astype(F32))
    out = pl.pallas_call(
        _ada_kernel,
        grid=(n // tn,),
        in_specs=[pl.BlockSpec((8, d), lambda j: (0, 0)),
                  pl.BlockSpec((d, tn), lambda j: (0, j))],
        out_specs=pl.BlockSpec((8, tn), lambda j: (0, j)),
        out_shape=jax.ShapeDtypeStruct((8, n), F32),
        compiler_params=_params("arbitrary"),
        name="ada_project",
    )(c8, ada_w)
    return out[:bsz]


def _rms(x, gain):
    return x * lax.rsqrt(jnp.mean(x * x, axis=-1, keepdims=True) + RMS_EPS) * gain


def _pre_kernel(x_ref, vec_ref, h_ref):
    h = _rms(x_ref[...], vec_ref[0:1, :]) * (1.0 + vec_ref[1:2, :]) + vec_ref[2:3, :]
    h_ref[...] = h.astype(h_ref.dtype)


def pre_norm(x, g_pre, scale, shift, tm=256):
    m, d = x.shape
    vec = jnp.zeros((8, d), F32).at[0].set(g_pre).at[1].set(scale).at[2].set(shift)
    return pl.pallas_call(
        _pre_kernel,
        grid=(m // tm,),
        in_specs=[pl.BlockSpec((tm, d), lambda i: (i, 0)),
                  pl.BlockSpec((8, d), lambda i: (0, 0))],
        out_specs=pl.BlockSpec((tm, d), lambda i: (i, 0)),
        out_shape=jax.ShapeDtypeStruct((m, d), BF16),
        compiler_params=_params("arbitrary"),
        name="pre_norm",
    )(x, vec)


def _post_kernel(y_ref, x_ref, vec_ref, xo_ref, *maybe_h_ref, res_w):
    yn = _rms(y_ref[...].astype(F32), vec_ref[0:1, :])
    xn = x_ref[...] + (res_w * vec_ref[1:2, :]) * yn
    xo_ref[...] = xn
    if maybe_h_ref:
        h = _rms(xn, vec_ref[2:3, :]) * (1.0 + vec_ref[3:4, :]) + vec_ref[4:5, :]
        maybe_h_ref[0][...] = h.astype(BF16)


def post_norm(y, x, g_post, gate, res_w, nxt=None, tm=256):
    m, d = x.shape
    vec = jnp.zeros((8, d), F32).at[0].set(g_post).at[1].set(gate)
    out_shape = [jax.ShapeDtypeStruct((m, d), F32)]
    out_specs = [pl.BlockSpec((tm, d), lambda i: (i, 0))]
    if nxt is not None:
        vec = vec.at[2].set(nxt[0]).at[3].set(nxt[1]).at[4].set(nxt[2])
        out_shape.append(jax.ShapeDtypeStruct((m, d), BF16))
        out_specs.append(pl.BlockSpec((tm, d), lambda i: (i, 0)))
    res = pl.pallas_call(
        functools.partial(_post_kernel, res_w=res_w),
        grid=(m // tm,),
        in_specs=[pl.BlockSpec((tm, d), lambda i: (i, 0)),
                  pl.BlockSpec((tm, d), lambda i: (i, 0)),
                  pl.BlockSpec((8, d), lambda i: (0, 0))],
        out_specs=out_specs,
        out_shape=out_shape,
        compiler_params=_params("arbitrary"),
        name="post_norm",
    )(y, x, vec)
    return (res[0], res[1]) if nxt is not None else (res[0], None)


def _mm_kernel(a_ref, w_ref, o_ref, *, tc):
    a = a_ref[...]
    for n0 in range(0, o_ref.shape[1], tc):
        o_ref[:, n0:n0 + tc] = jnp.dot(a, w_ref[:, n0:n0 + tc].astype(BF16),
                                       preferred_element_type=F32).astype(o_ref.dtype)


def matmul(a, w, lead, out_dtype, col0=0, ncols=None, tm=2048, tn=512):
    m, k = a.shape
    n = w.shape[-1] - col0 if ncols is None else ncols
    tm, tn = min(tm, m), min(tn, n)
    assert col0 % tn == 0 and n % tn == 0 and m % tm == 0
    j0 = col0 // tn
    return pl.pallas_call(
        functools.partial(_mm_kernel, tc=min(256, tn)),
        grid=(m // tm, n // tn),
        in_specs=[pl.BlockSpec((tm, k), lambda i, j: (i, 0), pipeline_mode=pl.Buffered(1)),
                  _wspec(lead, (k, tn), lambda i, j: (0, j + j0))],
        out_specs=pl.BlockSpec((tm, tn), lambda i, j: (i, j)),
        out_shape=jax.ShapeDtypeStruct((m, n), out_dtype),
        compiler_params=_params("arbitrary", "arbitrary"),
        name="matmul",
    )(a, w)


def _gateup_kernel(a_ref, w1_ref, w3_ref, o_ref, *, tc):
    a = a_ref[...]
    for n0 in range(0, o_ref.shape[1], tc):
        g = jnp.dot(a, w1_ref[:, n0:n0 + tc].astype(BF16), preferred_element_type=F32)
        u = jnp.dot(a, w3_ref[:, n0:n0 + tc].astype(BF16), preferred_element_type=F32)
        o_ref[:, n0:n0 + tc] = (g * jax.nn.sigmoid(g) * u).astype(o_ref.dtype)


def gate_up(a, w1, w3, lead, tm=1024, tn=512):
    m, k = a.shape
    n = w1.shape[-1]
    tm, tn = min(tm, m), min(tn, n)
    wspec = _wspec(lead, (k, tn), lambda i, j: (0, j))
    return pl.pallas_call(
        functools.partial(_gateup_kernel, tc=min(256, tn)),
        grid=(m // tm, pl.cdiv(n, tn)),
        in_specs=[pl.BlockSpec((tm, k), lambda i, j: (i, 0), pipeline_mode=pl.Buffered(1)), wspec, wspec],
        out_specs=pl.BlockSpec((tm, tn), lambda i, j: (i, j)),
        out_shape=jax.ShapeDtypeStruct((m, n), BF16),
        compiler_params=_params("arbitrary", "arbitrary"),
        name="gate_up",
    )(a, w1, w3)


def _mm_acc_kernel(*refs, nks, tails, tn):
    a_refs, w_ref, o_ref = refs[:len(nks)], refs[len(nks)], refs[len(nks) + 1]
    kk = pl.program_id(2)
    tk = w_ref.shape[0]

    @pl.when(kk == 0)
    def _():
        o_ref[...] = jnp.zeros_like(o_ref)

    def accumulate(a, valid):
        if valid < tk:
            a = jnp.where(lax.broadcasted_iota(jnp.int32, a.shape, 1) < valid, a, jnp.zeros_like(a))
        for n0 in range(0, o_ref.shape[1], tn):
            w = w_ref[:, n0:n0 + tn]
            if valid < tk:
                w = jnp.where(lax.broadcasted_iota(jnp.int32, w.shape, 0) < valid, w, 0.0)
            o_ref[:, n0:n0 + tn] += jnp.dot(a, w.astype(BF16), preferred_element_type=F32)

    start = 0
    for a_ref, nk, tail in zip(a_refs, nks, tails):
        local = kk - start
        if tail == tk:
            @pl.when(jnp.logical_and(local >= 0, local < nk))
            def _(a_ref=a_ref):
                accumulate(a_ref[...], tk)
        else:
            @pl.when(jnp.logical_and(local >= 0, local < nk - 1))
            def _(a_ref=a_ref):
                accumulate(a_ref[...], tk)

            @pl.when(local == nk - 1)
            def _(a_ref=a_ref, tail=tail):
                accumulate(a_ref[...], tail)
        start += nk


def matmul_acc(a_list, w, lead, tm=2048, tno=2048, tk=512):
    m = a_list[0].shape[0]
    n = w.shape[-1]
    tm, tno = min(tm, m), min(tno, n)
    assert all(a.shape[1] % tk == 0 for a in a_list[:-1]) and n % tno == 0 and m % tm == 0
    nks = tuple(pl.cdiv(a.shape[1], tk) for a in a_list)
    tails = tuple(a.shape[1] - (nk - 1) * tk for a, nk in zip(a_list, nks))
    starts = [sum(nks[:p]) for p in range(len(nks))]
    a_specs = [pl.BlockSpec((tm, tk), functools.partial(
        lambda i, j, kk, s0, nk: (i, jnp.clip(kk - s0, 0, nk - 1)), s0=s0, nk=nk))
        for s0, nk in zip(starts, nks)]
    return pl.pallas_call(
        functools.partial(_mm_acc_kernel, nks=nks, tails=tails, tn=min(512, tno)),
        grid=(m // tm, n // tno, sum(nks)),
        in_specs=a_specs + [_wspec(lead, (tk, tno), lambda i, j, kk: (kk, j))],
        out_specs=pl.BlockSpec((tm, tno), lambda i, j, kk: (i, j)),
        out_shape=jax.ShapeDtypeStruct((m, n), F32),
        compiler_params=_params("arbitrary", "arbitrary", "arbitrary"),
        name="matmul_acc",
    )(*a_list, w)


def _sb_kernel(q_ref, k_ref, v_ref, o_ref, *, t):
    i = pl.program_id(1)
    nch = q_ref.shape[0] // t
    qs = [q_ref[ch * t:(ch + 1) * t, :] for ch in range(nch)]
    row = lax.broadcasted_iota(jnp.int32, (t, t), 0)
    col = lax.broadcasted_iota(jnp.int32, (t, t), 1)
    suffix = (row > col).astype(BF16)
    scale = HEAD_DIM ** -0.5

    def cond(carry):
        step, cs, _ = carry
        top = functools.reduce(jnp.maximum, [jnp.max(c) for c in cs])
        return jnp.logical_and(nch * i + nch - 1 - step >= 0, top >= EXP_UNDERFLOW)

    def body(carry):
        step, cs, accs = carry
        new_c, new_acc = [], []
        for ch in range(nch):
            qblk = nch * i + ch
            j = qblk - step
            jc = jnp.maximum(j, 0)
            start = pl.multiple_of(jc * t, t)
            kj = k_ref[pl.ds(start, t), :]
            vj = v_ref[pl.ds(start, t), :]
            z = _nt(qs[ch], kj) * scale
            mask = (col - row) < jnp.where(j >= 0, (qblk - jc) * t, -2 * t)
            ls_pos = jnp.minimum(z, 0.0) - jnp.log(1.0 + jnp.exp(-jnp.abs(z)))
            lk = jnp.where(mask, ls_pos - z, 0.0)
            later = sum(jnp.dot(p, suffix, preferred_element_type=F32) for p in _split_bf16(lk))
            w = jnp.where(mask, jnp.exp(ls_pos + later + cs[ch]), 0.0)
            new_acc.append(accs[ch] + jnp.dot(w.astype(BF16), vj, preferred_element_type=F32))
            new_c.append(cs[ch] + jnp.sum(lk, axis=1, keepdims=True))
        return step + 1, tuple(new_c), tuple(new_acc)

    init = (jnp.int32(0), tuple(jnp.zeros((t, 1), F32) for _ in range(nch)),
            tuple(jnp.zeros((t, HEAD_DIM), F32) for _ in range(nch)))
    _, _, accs = lax.while_loop(cond, body, init)
    for ch in range(nch):
        o_ref[ch * t:(ch + 1) * t, :] = accs[ch].astype(o_ref.dtype)


def stick_breaking(proj, q_off, k_off, v_off, chains=2):
    s = proj.shape[0]
    t = min(SB_BLOCK, s)
    tq = min(chains * t, s)
    return pl.pallas_call(
        functools.partial(_sb_kernel, t=t),
        grid=(N_HEADS, s // tq),
        in_specs=[pl.BlockSpec((tq, HEAD_DIM), lambda h, i: (i, q_off + h)),
                  pl.BlockSpec((s, HEAD_DIM), lambda h, i: (0, k_off + h)),
                  pl.BlockSpec((s, HEAD_DIM), lambda h, i: (0, v_off + h))],
        out_specs=pl.BlockSpec((tq, HEAD_DIM), lambda h, i: (i, h)),
        out_shape=jax.ShapeDtypeStruct((s, HALF_WIDTH), BF16),
        compiler_params=_params("arbitrary", "arbitrary"),
        name="stick_breaking",
    )(proj, proj, proj)


def _s5_spread(rows, cols, row_period, col_period, row_div, col_div):
    r = lax.broadcasted_iota(jnp.int32, (rows, cols), 0)
    c = lax.broadcasted_iota(jnp.int32, (rows, cols), 1)
    same = jnp.logical_and(r // row_div == c // col_div, r % row_period == c % col_period)
    return same.astype(BF16)


def _s5_expand(compact, spread, rows_per_group, lanes_per_group):
    full = jnp.dot(compact, spread, preferred_element_type=F32)
    r = lax.broadcasted_iota(jnp.int32, full.shape, 0)
    c = lax.broadcasted_iota(jnp.int32, full.shape, 1)
    keep = (r // rows_per_group) % SSM_PACK == (c // lanes_per_group) % SSM_PACK
    return jnp.where(keep, full, 0.0).astype(BF16)


def _s5_kernel(u_ref, toep_ref, win_ref, vout_ref, ar_ref, ai_ref, y_ref, sin_ref, acc_ref):
    L, H, P = SSM_CHUNK, SSM_GROUP, SSM_STATE
    nc = u_ref.shape[0] // L
    jc = pl.program_id(1)
    per = acc_ref.shape[1] // HEAD_DIM
    xp = [jnp.concatenate([u_ref[pl.ds(2 * p, nc, stride=L), :], u_ref[pl.ds(2 * p + 1, nc, stride=L), :]],
                          axis=1).astype(BF16) for p in range(L // 2)]
    pair = lambda ref, p: ref[0, 0, 2 * p:2 * p + 2].reshape(2 * HEAD_DIM, ref.shape[-1])

    @pl.when(jc == 0)
    def _():
        nstate = sin_ref.shape[1]
        half = nstate // 2
        spread = _s5_spread(2 * P, nstate, P, P, P, half)
        b = sum(jnp.dot(xp[p], _s5_expand(pair(win_ref, p), spread, H, P), preferred_element_type=F32)
                for p in range(L // 2))
        er, ei = b[:, :half], b[:, half:]
        row = lax.broadcasted_iota(jnp.int32, er.shape, 0)
        for k in range(int(math.log2(nc))):
            sh = 1 << k
            sr = jnp.where(row >= sh, pltpu.roll(er, sh, axis=0), 0.0)
            si = jnp.where(row >= sh, pltpu.roll(ei, sh, axis=0), 0.0)
            ar, ai = ar_ref[0, k:k + 1, :], ai_ref[0, k:k + 1, :]
            er, ei = er + ar * sr - ai * si, ei + ar * si + ai * sr
        sin_ref[:, :half] = jnp.where(row >= 1, pltpu.roll(er, 1, axis=0), 0.0).astype(BF16)
        sin_ref[:, half:] = jnp.where(row >= 1, pltpu.roll(ei, 1, axis=0), 0.0).astype(BF16)

    spread = _s5_spread(per * H, per * HEAD_DIM, H, H, H, HEAD_DIM)
    acc_ref[...] = jnp.dot(sin_ref[...], _s5_expand(vout_ref[0, 0], spread, P, H),
                           preferred_element_type=F32)
    for p in range(L // 2):
        @pl.when(2 * p < (jc + 1) * per)
        def _(p=p):
            acc_ref[...] += jnp.dot(xp[p], _s5_expand(pair(toep_ref, p), spread, H, H),
                                    preferred_element_type=F32)
    for t in range(per):
        y_ref[pl.ds(jc * per + t, nc, stride=L), :] = acc_ref[:, t * HEAD_DIM:(t + 1) * HEAD_DIM]


def s5_tables(a_re, a_im, b_re, b_im, c_re, c_im, d_skip, log_dt, ncol):
    L, P, H, G, K8 = SSM_CHUNK, SSM_STATE, SSM_GROUP, SSM_GROUPS, SSM_PACK
    GC = G // K8
    per = L // ncol
    dt = jnp.exp(log_dt.astype(F32))[:, None]
    ar, ai = a_re.astype(F32), a_im.astype(F32)
    mag = jnp.exp(ar * dt)
    lr, li = mag * jnp.cos(ai * dt), mag * jnp.sin(ai * dt)
    den = ar * ar + ai * ai
    nr, ni = lr - 1.0, li
    cr = (nr * ar + ni * ai) / den
    ci = (ni * ar - nr * ai) / den
    br, bi = b_re.astype(F32), b_im.astype(F32)
    bbr = cr[..., None] * br - ci[..., None] * bi
    bbi = cr[..., None] * bi + ci[..., None] * br
    pr, pi = [jnp.ones_like(lr)], [jnp.zeros_like(li)]
    for _ in range(L):
        pr, pi = pr + [pr[-1] * lr - pi[-1] * li], pi + [pr[-1] * li + pi[-1] * lr]
    pr, pi = jnp.stack(pr, 1), jnp.stack(pi, 1)
    ccr, cci = c_re.astype(F32), c_im.astype(F32)
    clr = ccr[:, None] * pr[:, :L, None, :] - cci[:, None] * pi[:, :L, None, :]
    cli = ccr[:, None] * pi[:, :L, None, :] + cci[:, None] * pr[:, :L, None, :]
    kern = (jnp.einsum('gdnp,gph->gdnh', clr, bbr, precision='highest')
            - jnp.einsum('gdnp,gph->gdnh', cli, bbi, precision='highest'))
    kern = kern.at[:, 0].add(jnp.eye(H, dtype=F32)[None] * d_skip.astype(F32).reshape(G, H)[:, :, None])
    tau = jnp.arange(L)
    diff = tau[None, :] - tau[:, None]
    kt = kern[:, jnp.clip(diff, 0, L - 1)]
    kt = jnp.where((diff >= 0)[None, :, :, None, None], kt, 0.0)
    toep = kt.reshape(GC, K8, L, ncol, per, H, H).transpose(0, 3, 2, 1, 6, 4, 5)
    toep = toep.reshape(GC, ncol, L, K8 * H, per * H)
    qr, qi = pr[:, L - 1 - tau], pi[:, L - 1 - tau]
    bt_r, bt_i = bbr.transpose(0, 2, 1)[:, None], bbi.transpose(0, 2, 1)[:, None]
    wre = qr[:, :, None, :] * bt_r - qi[:, :, None, :] * bt_i
    wim = qr[:, :, None, :] * bt_i + qi[:, :, None, :] * bt_r
    win = jnp.concatenate([wre, wim], -1)
    win = win.reshape(GC, K8, L, H, 2 * P).transpose(0, 2, 1, 3, 4).reshape(GC, 1, L, K8 * H, 2 * P)
    pr1, pi1 = pr[:, 1:], pi[:, 1:]
    vre = ccr[:, None] * pr1[:, :, None, :] - cci[:, None] * pi1[:, :, None, :]
    vim = ccr[:, None] * pi1[:, :, None, :] + cci[:, None] * pr1[:, :, None, :]
    vout = jnp.stack([vre, -vim], 1)
    vout = vout.reshape(GC, K8, 2, ncol, per, H, P).transpose(0, 3, 2, 1, 6, 4, 5)
    vout = vout.reshape(GC, ncol, 2 * K8 * P, per * H)
    zr, zi = [pr[:, L]], [pi[:, L]]
    for _ in range(15):
        zr, zi = zr + [zr[-1] * zr[-1] - zi[-1] * zi[-1]], zi + [2.0 * zr[-1] * zi[-1]]
    pack = lambda z: jnp.stack(z, 1).reshape(GC, K8, 16, P).transpose(0, 2, 1, 3).reshape(GC, 16, K8 * P)
    return toep.astype(BF16), win.astype(BF16), vout.astype(BF16), pack(zr), pack(zi)


def s5_scan(u, s5_params, ncol=4):
    toep, win, vout, zr, zi = s5_tables(*s5_params, ncol=ncol)
    s, width = u.shape
    L, K8, P = SSM_CHUNK, SSM_PACK, SSM_STATE
    gc = width // HEAD_DIM
    cw = (L // ncol) * HEAD_DIM
    nstate = 2 * K8 * P
    return pl.pallas_call(
        _s5_kernel,
        grid=(gc, ncol),
        in_specs=[pl.BlockSpec((s, HEAD_DIM), lambda g, j: (0, g)),
                  pl.BlockSpec((1, 1) + toep.shape[2:], lambda g, j: (g, j, 0, 0, 0)),
                  pl.BlockSpec((1, 1) + win.shape[2:], lambda g, j: (g, 0, 0, 0, 0)),
                  pl.BlockSpec((1, 1) + vout.shape[2:], lambda g, j: (g, j, 0, 0)),
                  pl.BlockSpec((1, 16, nstate // 2), lambda g, j: (g, 0, 0)),
                  pl.BlockSpec((1, 16, nstate // 2), lambda g, j: (g, 0, 0))],
        out_specs=pl.BlockSpec((s, HEAD_DIM), lambda g, j: (0, g)),
        out_shape=jax.ShapeDtypeStruct((s, width), F32),
        scratch_shapes=[pltpu.VMEM((s // L, nstate), BF16), pltpu.VMEM((s // L, cw), F32)],
        compiler_params=_params("arbitrary", "arbitrary"),
        name="s5_scan",
    )(u, toep, win, vout, zr, zi)


def _glu_kernel(y_ref, yj_ref, w_ref, o_ref):
    yg = jax.nn.gelu(y_ref[...])
    z = jnp.dot(yg.astype(BF16), w_ref[...].astype(BF16), preferred_element_type=F32)
    o_ref[...] = (jax.nn.gelu(yj_ref[...]) * jax.nn.sigmoid(z)).astype(o_ref.dtype)


def gelu_glu(y, glu_w, lead, tm=512, tn=512):
    m, k = y.shape
    tm = min(tm, m)
    return pl.pallas_call(
        _glu_kernel,
        grid=(m // tm, k // tn),
        in_specs=[pl.BlockSpec((tm, k), lambda i, j: (i, 0)),
                  pl.BlockSpec((tm, tn), lambda i, j: (i, j)),
                  _wspec(lead, (k, tn), lambda i, j: (0, j))],
        out_specs=pl.BlockSpec((tm, tn), lambda i, j: (i, j)),
        out_shape=jax.ShapeDtypeStruct((m, k), BF16),
        compiler_params=_params("arbitrary", "arbitrary"),
        name="gelu_glu",
    )(y, y, glu_w)


def _hgrn_kernel(q_ref, f_ref, i_ref, g_ref, lb_ref, nw_ref, o_ref, state_ref):
    @pl.when(pl.program_id(1) == 0)
    def _():
        state_ref[...] = jnp.zeros_like(state_ref)

    tc, d = q_ref.shape
    sub = HGRN_SUB
    nblk = tc // sub
    lb = lb_ref[...]
    q = q_ref[...]
    v = i_ref[...]
    f = lb + (1.0 - lb) * jax.nn.sigmoid(f_ref[...])
    k = 1.0 - f
    r_i = lax.broadcasted_iota(jnp.int32, (tc, tc), 0)
    c_i = lax.broadcasted_iota(jnp.int32, (tc, tc), 1)
    tril = (c_i <= r_i).astype(BF16)
    gcum = sum(jnp.dot(tril, p, preferred_element_type=F32)
               for p in _split_bf16(jnp.log(f), parts=3))
    gtot = gcum[tc - 1:tc, :]
    state = state_ref[...]
    o = _nt((q * jnp.exp(gcum)).astype(BF16), state.astype(BF16))

    g3, q3, k3, v3 = (x.reshape(nblk, sub, d) for x in (gcum, q, k, v))
    s_idx = lax.broadcasted_iota(jnp.int32, (nblk, sub, d), 1)
    rows = []
    for t in range(sub):
        dec = jnp.exp(jnp.where(s_idx <= t, g3[:, t:t + 1] - g3, -jnp.inf))
        a = jnp.sum(q3[:, t:t + 1] * k3 * dec, axis=2, keepdims=True)
        rows.append(jnp.sum(a * v3, axis=1, keepdims=True))
    o = o + jnp.concatenate(rows, axis=1).reshape(tc, d)

    v16 = v.astype(BF16)
    b = sub
    while 2 * b <= tc:
        m = tc // (2 * b)
        g4, q4, k4 = (x.reshape(m, 2 * b, d) for x in (gcum, q, k))
        ref = g4[:, b:b + 1]
        qd = (q4[:, b:] * jnp.exp(g4[:, b:] - ref)).astype(BF16)
        kd = (k4[:, :b] * jnp.exp(ref - g4[:, :b])).astype(BF16)
        att = jnp.einsum('mqd,mkd->mqk', qd, kd, preferred_element_type=F32)
        ob = jnp.einsum('mqk,mkd->mqd', att.astype(BF16), v16.reshape(m, 2 * b, d)[:, :b],
                        preferred_element_type=F32)
        o = o + jnp.concatenate([jnp.zeros_like(ob), ob], axis=1).reshape(tc, d)
        b *= 2

    kdec = (k * jnp.exp(gtot - gcum)).astype(BF16)
    state_ref[...] = state * jnp.exp(gtot) + _tn(v16, kdec)
    on = o * lax.rsqrt(jnp.mean(o * o, axis=-1, keepdims=True) + RMS_EPS) * nw_ref[...]
    gate = g_ref[...]
    o_ref[...] = (on * (gate * jax.nn.sigmoid(gate))).astype(o_ref.dtype)


def hgrn2(proj, lb, norm_w, offs, tc=512):
    s = proj.shape[0]
    tc = min(tc, s)
    specs = [pl.BlockSpec((tc, HEAD_DIM), functools.partial(lambda h, i, o: (i, o + h), o=o)) for o in offs]
    return pl.pallas_call(
        _hgrn_kernel,
        grid=(N_HEADS, s // tc),
        in_specs=specs + [pl.BlockSpec((1, HEAD_DIM), lambda h, i: (0, h)),
                          pl.BlockSpec((1, HEAD_DIM), lambda h, i: (0, 0))],
        out_specs=pl.BlockSpec((tc, HEAD_DIM), lambda h, i: (i, h)),
        out_shape=jax.ShapeDtypeStruct((s, HALF_WIDTH), BF16),
        scratch_shapes=[pltpu.VMEM((HEAD_DIM, HEAD_DIM), F32)],
        compiler_params=_params("arbitrary", "arbitrary"),
        name="hgrn2",
    )(proj, proj, proj, proj, lb.reshape(1, HALF_WIDTH).astype(F32), norm_w.reshape(1, HEAD_DIM).astype(F32))


def rope_tables(s):
    half = ROPE_DIM // 2
    inv = ROPE_THETA ** (-jnp.arange(half, dtype=F32) / half)
    ang = jnp.arange(s).astype(F32)[:, None] * inv[None, :]
    cos, sin = jnp.cos(ang), jnp.sin(ang)
    one = jnp.ones((s, HEAD_DIM - ROPE_DIM), F32)
    zero = jnp.zeros((s, HEAD_DIM - ROPE_DIM), F32)
    z16 = jnp.zeros((s, half), F32)
    c = jnp.concatenate([cos, cos, one], 1)
    s_up = jnp.concatenate([z16, sin, zero], 1)
    s_dn = jnp.concatenate([-sin, z16, zero], 1)
    return c, s_up, s_dn


def _rope(x, c, s_up, s_dn):
    half = ROPE_DIM // 2
    return (x * c + pltpu.roll(x, half, axis=1) * s_up
            + pltpu.roll(x, HEAD_DIM - half, axis=1) * s_dn)


def _moba_prep_kernel(k_ref, v_ref, c_ref, su_ref, sd_ref, kr_ref, vt_ref, km_ref):
    c, su, sd = c_ref[...], su_ref[...], sd_ref[...]
    for h in range(N_HEADS):
        sl = slice(h * HEAD_DIM, (h + 1) * HEAD_DIM)
        kr = _rope(k_ref[:, sl], c, su, sd)
        kr_ref[:, sl] = kr.astype(BF16)
        km_ref[0, :, sl] = jnp.mean(kr, axis=0, keepdims=True)
    vt_ref[0] = v_ref[...].T.astype(BF16)


def moba_prep(proj, k_blk, v_blk, tables):
    s = proj.shape[0]
    nb = s // MOBA_BLOCK
    tab = pl.BlockSpec((MOBA_BLOCK, HEAD_DIM), lambda n: (n, 0))
    kr, vt, km = pl.pallas_call(
        _moba_prep_kernel,
        grid=(nb,),
        in_specs=[pl.BlockSpec((MOBA_BLOCK, HALF_WIDTH), lambda n: (n, k_blk)),
                  pl.BlockSpec((MOBA_BLOCK, HALF_WIDTH), lambda n: (n, v_blk)),
                  tab, tab, tab],
        out_specs=[pl.BlockSpec((MOBA_BLOCK, HALF_WIDTH), lambda n: (n, 0)),
                   pl.BlockSpec((1, HALF_WIDTH, MOBA_BLOCK), lambda n: (n, 0, 0)),
                   pl.BlockSpec((1, 1, HALF_WIDTH), lambda n: (n, 0, 0))],
        out_shape=[jax.ShapeDtypeStruct((s, HALF_WIDTH), BF16),
                   jax.ShapeDtypeStruct((nb, HALF_WIDTH, MOBA_BLOCK), BF16),
                   jax.ShapeDtypeStruct((nb, 1, HALF_WIDTH), F32)],
        compiler_params=_params("arbitrary"),
        name="moba_prep",
    )(proj, proj, *tables)
    return kr, vt, km.reshape(nb, HALF_WIDTH)


def _moba_kernel(q_ref, c_ref, su_ref, sd_ref, k_ref, vt_ref, km_ref, o_ref, sel_ref, *, unroll):
    own = pl.program_id(1)
    blk, d = MOBA_BLOCK, HEAD_DIM
    nb = km_ref.shape[0]
    heads = q_ref.shape[1] // d
    cols = lambda hh: slice(hh * d, (hh + 1) * d)
    c, su, sd = c_ref[...], su_ref[...], sd_ref[...]
    o0 = pl.multiple_of(own * blk, blk)

    qs, init = [], []
    for hh in range(heads):
        qf = _rope(q_ref[:, cols(hh)], c, su, sd)
        q = (qf * d ** -0.5).astype(BF16)
        qs.append(q)
        gate = _nt(km_ref[:, cols(hh)], qf, precision=lax.Precision.HIGHEST)
        row = lax.broadcasted_iota(jnp.int32, gate.shape, 0)
        g = jnp.where(row < own, gate, NEG)
        sel = jnp.zeros(gate.shape, F32)
        for j in range(MOBA_TOPK):
            m = jnp.max(g, axis=0, keepdims=True)
            idx = jnp.min(jnp.where(g == m, row, nb), axis=0, keepdims=True)
            pick = row == idx
            sel = jnp.where(pick, jnp.where(j < own, 1.0, 0.0), sel)
            g = jnp.where(pick, -jnp.inf, g)
        sel_ref[hh] = sel
        s0 = _nt(k_ref[pl.ds(o0, blk), cols(hh)], q)
        kpos = lax.broadcasted_iota(jnp.int32, s0.shape, 0)
        qpos = lax.broadcasted_iota(jnp.int32, s0.shape, 1)
        s0 = jnp.where(kpos <= qpos, s0, NEG)
        m0 = jnp.max(s0, axis=0, keepdims=True)
        p0 = jnp.exp(s0 - m0)
        l0 = jnp.sum(p0, axis=0, keepdims=True)
        acc0 = jnp.dot(vt_ref[own, cols(hh), :], p0.astype(BF16), preferred_element_type=F32)
        init.append((m0, l0, acc0))

    def body(it, carry):
        out = []
        for hh in range(heads):
            m, l, acc = carry[hh]
            scores = []
            for u in range(unroll):
                n = it * unroll + u
                start = pl.multiple_of(n * blk, blk)
                sn = _nt(k_ref[pl.ds(start, blk), cols(hh)], qs[hh])
                scores.append(jnp.where(sel_ref[hh, pl.ds(n, 1), :] > 0.0, sn, NEG))
            m_new = functools.reduce(jnp.maximum, [jnp.max(sn, axis=0, keepdims=True) for sn in scores], m)
            alpha = jnp.exp(m - m_new)
            l, acc = alpha * l, alpha * acc
            for u, sn in enumerate(scores):
                p = jnp.exp(sn - m_new)
                l = l + jnp.sum(p, axis=0, keepdims=True)
                acc = acc + jnp.dot(vt_ref[it * unroll + u, cols(hh), :], p.astype(BF16),
                                    preferred_element_type=F32)
            out.append((m_new, l, acc))
        return tuple(out)

    trips = (own + unroll - 1) // unroll
    final = lax.fori_loop(0, trips, body, tuple(init))
    for hh in range(heads):
        _, l, acc = final[hh]
        o_ref[:, cols(hh)] = (acc / l).T.astype(o_ref.dtype)


def moba(proj, q_off, kr, vt, kmean, tables, unroll=4, heads=2):
    s = proj.shape[0]
    nb = s // MOBA_BLOCK
    assert nb % unroll == 0 and q_off % heads == 0 and N_HEADS % heads == 0
    w = heads * HEAD_DIM
    tab = pl.BlockSpec((MOBA_BLOCK, HEAD_DIM), lambda h, i: (i, 0))
    return pl.pallas_call(
        functools.partial(_moba_kernel, unroll=unroll),
        grid=(N_HEADS // heads, nb),
        in_specs=[pl.BlockSpec((MOBA_BLOCK, w), lambda h, i: (i, q_off // heads + h)),
                  tab, tab, tab,
                  pl.BlockSpec((s, w), lambda h, i: (0, h)),
                  pl.BlockSpec((nb, w, MOBA_BLOCK), lambda h, i: (0, h, 0)),
                  pl.BlockSpec((nb, w), lambda h, i: (0, h))],
        out_specs=pl.BlockSpec((MOBA_BLOCK, w), lambda h, i: (i, h)),
        out_shape=jax.ShapeDtypeStruct((s, HALF_WIDTH), BF16),
        scratch_shapes=[pltpu.VMEM((heads, nb, MOBA_BLOCK), F32)],
        compiler_params=_params("arbitrary", "arbitrary"),
        name="moba",
    )(proj, *tables, kr, vt, kmean)


def _even_mixer(h, w_in, w_out, e, s5_params, glu_w):
    qkv = matmul(h, w_in, (e,), BF16, col0=0, ncols=3 * HALF_WIDTH)
    u = matmul(h, w_in, (e,), F32, col0=3 * HALF_WIDTH, ncols=HALF_WIDTH)
    o_a = stick_breaking(qkv, 0, N_HEADS, 2 * N_HEADS)
    o_b = gelu_glu(s5_scan(u, s5_params), glu_w, (e,))
    return matmul_acc([o_a, o_b], w_out, (e,))


def _odd_mixer(h, w_in, w_out, o, lb, norm_w):
    s = h.shape[0]
    proj = matmul(h, w_in, (o,), F32)
    o_c = hgrn2(proj, lb, norm_w, (0, N_HEADS, 2 * N_HEADS, 3 * N_HEADS))
    tables = rope_tables(s)
    kr, vt, kmean = moba_prep(proj, 5, 6, tables)
    o_d = moba(proj, 4 * N_HEADS, kr, vt, kmean, tables)
    return matmul_acc([o_c, o_d], w_out, (o,))


def kernel(x, c, ada_w, ada_table, norm_pre, norm_post, ffn_w1, ffn_w3, ffn_w2, ev_w_in, ev_w_out,
           s5_a_re, s5_a_im, s5_b_re, s5_b_im, s5_c_re, s5_c_im, s5_d, s5_log_dt, s5_glu_w,
           od_w_in, od_w_out, hgrn_lb, hgrn_norm_w):
    bsz, seq, d = x.shape
    depth = ada_table.shape[0]
    mod_shared = ada_project(c, ada_w).reshape(bsz, 9, d)
    lb_cum = jnp.cumsum(jax.nn.softmax(hgrn_lb.astype(F32), axis=0), axis=0)
    lb_all = lb_cum - lb_cum[:1]

    outs = []
    for b in range(bsz):
        xb = x[b]
        mods = [mod_shared[b] + ada_table[layer] for layer in range(depth)]
        subs = [(layer, slot) for layer in range(depth) for slot in range(3)]
        first = subs[0]
        h = pre_norm(xb, norm_pre[first[0], first[1]], mods[first[0]][1], mods[first[0]][0])
        for n, (layer, slot) in enumerate(subs):
            mod = mods[layer]
            if slot == 1:
                if layer % 2 == 0:
                    e = layer // 2
                    y = _even_mixer(h, ev_w_in, ev_w_out, e,
                                    (s5_a_re[e], s5_a_im[e], s5_b_re[e], s5_b_im[e], s5_c_re[e], s5_c_im[e],
                                     s5_d[e], s5_log_dt[e]), s5_glu_w)
                else:
                    o = layer // 2
                    y = _odd_mixer(h, od_w_in, od_w_out, o, lb_all[layer], hgrn_norm_w[o])
                res_w = 1.0
            else:
                f = slot // 2
                g = gate_up(h, ffn_w1, ffn_w3, (layer, f))
                y = matmul_acc([g], ffn_w2, (layer, f))
                res_w = 0.5
            nxt = None
            if n + 1 < len(subs):
                nl, ns = subs[n + 1]
                nxt = (norm_pre[nl, ns], mods[nl][3 * ns + 1], mods[nl][3 * ns])
            xb, h = post_norm(y, xb, norm_post[layer, slot], mod[3 * slot + 2], res_w, nxt)
        outs.append(xb)
    return jnp.stack(outs, axis=0)
```

```python
import functools
import math

import jax
import jax.numpy as jnp
from jax import lax
from jax.experimental import pallas as pl
from jax.experimental.pallas import tpu as pltpu

F32 = jnp.float32
BF16 = jnp.bfloat16

HEAD_DIM = 128
N_HEADS = 16
HALF_WIDTH = N_HEADS * HEAD_DIM
SSM_GROUP = 16
SSM_GROUPS = HALF_WIDTH // SSM_GROUP
SSM_STATE = 64
SSM_CHUNK = 16
SSM_PACK = HEAD_DIM // SSM_GROUP
HGRN_SUB = 16
SB_BLOCK = 256
MOBA_BLOCK = 256
MOBA_TOPK = 3
ROPE_THETA = 500000.0
ROPE_DIM = HEAD_DIM // 4
RMS_EPS = 1e-6
NEG = -1e30
EXP_UNDERFLOW = -104.0

VMEM_LIMIT = 56 * 1024 * 1024


def _params(*sem):
    return pltpu.CompilerParams(dimension_semantics=sem, vmem_limit_bytes=VMEM_LIMIT)


def _nt(a, b, **kw):
    return lax.dot_general(a, b, (((1,), (1,)), ((), ())), preferred_element_type=F32, **kw)


def _tn(a, b):
    return lax.dot_general(a, b, (((0,), (0,)), ((), ())), preferred_element_type=F32)


def _split_bf16(x, parts=2):
    out = []
    for _ in range(parts):
        p = x.astype(BF16)
        out.append(p)
        x = x - p.astype(F32)
    return out


def _wspec(lead, block, index_map):
    lead = tuple(lead)
    return pl.BlockSpec((None,) * len(lead) + tuple(block), lambda *g: lead + tuple(index_map(*g)))


def _ada_kernel(c_ref, w_ref, o_ref):
    c = c_ref[...]
    a = c * jax.nn.sigmoid(c)
    o_ref[...] = jnp.dot(a, w_ref[...], precision=lax.Precision.HIGHEST,
                         preferred_element_type=F32)


def ada_project(c, ada_w, tn=1024):
    d, n = ada_w.shape
    bsz = c.shape[0]
    assert bsz <= 8
    c8 = jnp.zeros((8, d), F32).at[:bsz].set(c.astype(F32))
    out = pl.pallas_call(
        _ada_kernel,
        grid=(n // tn,),
        in_specs=[pl.BlockSpec((8, d), lambda j: (0, 0)),
                  pl.BlockSpec((d, tn), lambda j: (0, j))],
        out_specs=pl.BlockSpec((8, tn), lambda j: (0, j)),
        out_shape=jax.ShapeDtypeStruct((8, n), F32),
        compiler_params=_params("arbitrary"),
        name="ada_project",
    )(c8, ada_w)
    return out[:bsz]


def _rms(x, gain):
    return x * lax.rsqrt(jnp.mean(x * x, axis=-1, keepdims=True) + RMS_EPS) * gain


def _pre_kernel(x_ref, vec_ref, h_ref):
    h = _rms(x_ref[...], vec_ref[0:1, :]) * (1.0 + vec_ref[1:2, :]) + vec_ref[2:3, :]
    h_ref[...] = h.astype(h_ref.dtype)


def pre_norm(x, g_pre, scale, shift, tm=256):
    m, d = x.shape
    vec = jnp.zeros((8, d), F32).at[0].set(g_pre).at[1].set(scale).at[2].set(shift)
    return pl.pallas_call(
        _pre_kernel,
        grid=(m // tm,),
        in_specs=[pl.BlockSpec((tm, d), lambda i: (i, 0)),
                  pl.BlockSpec((8, d), lambda i: (0, 0))],
        out_specs=pl.BlockSpec((tm, d), lambda i: (i, 0)),
        out_shape=jax.ShapeDtypeStruct((m, d), BF16),
        compiler_params=_params("arbitrary"),
        name="pre_norm",
    )(x, vec)


def _post_kernel(y_ref, x_ref, vec_ref, xo_ref, *maybe_h_ref, res_w):
    yn = _rms(y_ref[...].astype(F32), vec_ref[0:1, :])
    xn = x_ref[...] + (res_w * vec_ref[1:2, :]) * yn
    xo_ref[...] = xn
    if maybe_h_ref:
        h = _rms(xn, vec_ref[2:3, :]) * (1.0 + vec_ref[3:4, :]) + vec_ref[4:5, :]
        maybe_h_ref[0][...] = h.astype(BF16)


def post_norm(y, x, g_post, gate, res_w, nxt=None, tm=256):
    m, d = x.shape
    vec = jnp.zeros((8, d), F32).at[0].set(g_post).at[1].set(gate)
    out_shape = [jax.ShapeDtypeStruct((m, d), F32)]
    out_specs = [pl.BlockSpec((tm, d), lambda i: (i, 0))]
    if nxt is not None:
        vec = vec.at[2].set(nxt[0]).at[3].set(nxt[1]).at[4].set(nxt[2])
        out_shape.append(jax.ShapeDtypeStruct((m, d), BF16))
        out_specs.append(pl.BlockSpec((tm, d), lambda i: (i, 0)))
    res = pl.pallas_call(
        functools.partial(_post_kernel, res_w=res_w),
        grid=(m // tm,),
        in_specs=[pl.BlockSpec((tm, d), lambda i: (i, 0)),
                  pl.BlockSpec((tm, d), lambda i: (i, 0)),
                  pl.BlockSpec((8, d), lambda i: (0, 0))],
        out_specs=out_specs,
        out_shape=out_shape,
        compiler_params=_params("arbitrary"),
        name="post_norm",
    )(y, x, vec)
    return (res[0], res[1]) if nxt is not None else (res[0], None)


def _mm_kernel(a_ref, w_ref, o_ref, *, tc):
    a = a_ref[...]
    for n0 in range(0, o_ref.shape[1], tc):
        o_ref[:, n0:n0 + tc] = jnp.dot(a, w_ref[:, n0:n0 + tc].astype(BF16),
                                       preferred_element_type=F32).astype(o_ref.dtype)


def matmul(a, w, lead, out_dtype, col0=0, ncols=None, tm=2048, tn=512):
    m, k = a.shape
    n = w.shape[-1] - col0 if ncols is None else ncols
    tm, tn = min(tm, m), min(tn, n)
    assert col0 % tn == 0 and n % tn == 0 and m % tm == 0
    j0 = col0 // tn
    return pl.pallas_call(
        functools.partial(_mm_kernel, tc=min(256, tn)),
        grid=(m // tm, n // tn),
        in_specs=[pl.BlockSpec((tm, k), lambda i, j: (i, 0), pipeline_mode=pl.Buffered(1)),
                  _wspec(lead, (k, tn), lambda i, j: (0, j + j0))],
        out_specs=pl.BlockSpec((tm, tn), lambda i, j: (i, j)),
        out_shape=jax.ShapeDtypeStruct((m, n), out_dtype),
        compiler_params=_params("arbitrary", "arbitrary"),
        name="matmul",
    )(a, w)


def _mm_cat_kernel(*refs, chunks):
    a_refs, w_ref, o_ref = refs[:len(chunks)], refs[len(chunks)], refs[len(chunks) + 1]
    acc = None
    row = 0
    for a_ref, kc in zip(a_refs, chunks):
        for k0 in range(0, a_ref.shape[1], kc):
            part = jnp.dot(a_ref[:, k0:k0 + kc], w_ref[row + k0:row + k0 + kc, :].astype(BF16),
                           preferred_element_type=F32)
            acc = part if acc is None else acc + part
        row += a_ref.shape[1]
    o_ref[...] = acc.astype(o_ref.dtype)


def _k_chunk(k, limit=6144):
    best = HEAD_DIM
    for c in range(HEAD_DIM, min(k, limit) + 1, HEAD_DIM):
        if k % c == 0:
            best = c
    return best


def matmul_cat(a_list, w, lead, out_dtype, tm=1024, tn=256):
    m = a_list[0].shape[0]
    k_total, n = w.shape[-2], w.shape[-1]
    assert sum(a.shape[1] for a in a_list) == k_total and m % tm == 0 and n % tn == 0
    chunks = tuple(_k_chunk(a.shape[1]) for a in a_list)
    a_specs = [pl.BlockSpec((tm, a.shape[1]), lambda i, j: (i, 0), pipeline_mode=pl.Buffered(1))
               for a in a_list]
    return pl.pallas_call(
        functools.partial(_mm_cat_kernel, chunks=chunks),
        grid=(m // tm, n // tn),
        in_specs=a_specs + [_wspec(lead, (k_total, tn), lambda i, j: (0, j))],
        out_specs=pl.BlockSpec((tm, tn), lambda i, j: (i, j)),
        out_shape=jax.ShapeDtypeStruct((m, n), out_dtype),
        compiler_params=_params("arbitrary", "arbitrary"),
        name="matmul_cat",
    )(*a_list, w)


def _gateup_kernel(a_ref, w1_ref, w3_ref, o_ref, *, tc):
    a = a_ref[...]
    for n0 in range(0, o_ref.shape[1], tc):
        g = jnp.dot(a, w1_ref[:, n0:n0 + tc].astype(BF16), preferred_element_type=F32)
        u = jnp.dot(a, w3_ref[:, n0:n0 + tc].astype(BF16), preferred_element_type=F32)
        o_ref[:, n0:n0 + tc] = (g * jax.nn.sigmoid(g) * u).astype(o_ref.dtype)


def gate_up(a, w1, w3, lead, tm=1024, tn=512):
    m, k = a.shape
    n = w1.shape[-1]
    tm, tn = min(tm, m), min(tn, n)
    wspec = _wspec(lead, (k, tn), lambda i, j: (0, j))
    return pl.pallas_call(
        functools.partial(_gateup_kernel, tc=min(256, tn)),
        grid=(m // tm, pl.cdiv(n, tn)),
        in_specs=[pl.BlockSpec((tm, k), lambda i, j: (i, 0), pipeline_mode=pl.Buffered(1)), wspec, wspec],
        out_specs=pl.BlockSpec((tm, tn), lambda i, j: (i, j)),
        out_shape=jax.ShapeDtypeStruct((m, n), BF16),
        compiler_params=_params("arbitrary", "arbitrary"),
        name="gate_up",
    )(a, w1, w3)


def _sb_kernel(q_ref, k_ref, v_ref, o_ref, *, t):
    i = pl.program_id(1)
    nch = q_ref.shape[0] // t
    qs = [q_ref[ch * t:(ch + 1) * t, :] for ch in range(nch)]
    row = lax.broadcasted_iota(jnp.int32, (t, t), 0)
    col = lax.broadcasted_iota(jnp.int32, (t, t), 1)
    suffix = (row > col).astype(BF16)
    scale = HEAD_DIM ** -0.5

    def cond(carry):
        step, cs, _ = carry
        top = functools.reduce(jnp.maximum, [jnp.max(c) for c in cs])
        return jnp.logical_and(nch * i + nch - 1 - step >= 0, top >= EXP_UNDERFLOW)

    def body(carry):
        step, cs, accs = carry
        new_c, new_acc = [], []
        for ch in range(nch):
            qblk = nch * i + ch
            j = qblk - step
            jc = jnp.maximum(j, 0)
            start = pl.multiple_of(jc * t, t)
            kj = k_ref[pl.ds(start, t), :]
            vj = v_ref[pl.ds(start, t), :]
            z = _nt(qs[ch], kj) * scale
            mask = (col - row) < jnp.where(j >= 0, (qblk - jc) * t, -2 * t)
            ls_pos = jnp.minimum(z, 0.0) - jnp.log(1.0 + jnp.exp(-jnp.abs(z)))
            lk = jnp.where(mask, ls_pos - z, 0.0)
            later = sum(jnp.dot(p, suffix, preferred_element_type=F32) for p in _split_bf16(lk))
            w = jnp.where(mask, jnp.exp(ls_pos + later + cs[ch]), 0.0)
            new_acc.append(accs[ch] + jnp.dot(w.astype(BF16), vj, preferred_element_type=F32))
            new_c.append(cs[ch] + jnp.sum(lk, axis=1, keepdims=True))
        return step + 1, tuple(new_c), tuple(new_acc)

    init = (jnp.int32(0), tuple(jnp.zeros((t, 1), F32) for _ in range(nch)),
            tuple(jnp.zeros((t, HEAD_DIM), F32) for _ in range(nch)))
    _, _, accs = lax.while_loop(cond, body, init)
    for ch in range(nch):
        o_ref[ch * t:(ch + 1) * t, :] = accs[ch].astype(o_ref.dtype)


def stick_breaking(proj, q_off, k_off, v_off, chains=2):
    s = proj.shape[0]
    t = min(SB_BLOCK, s)
    tq = min(chains * t, s)
    return pl.pallas_call(
        functools.partial(_sb_kernel, t=t),
        grid=(N_HEADS, s // tq),
        in_specs=[pl.BlockSpec((tq, HEAD_DIM), lambda h, i: (i, q_off + h)),
                  pl.BlockSpec((s, HEAD_DIM), lambda h, i: (0, k_off + h)),
                  pl.BlockSpec((s, HEAD_DIM), lambda h, i: (0, v_off + h))],
        out_specs=pl.BlockSpec((tq, HEAD_DIM), lambda h, i: (i, h)),
        out_shape=jax.ShapeDtypeStruct((s, HALF_WIDTH), BF16),
        compiler_params=_params("arbitrary", "arbitrary"),
        name="stick_breaking",
    )(proj, proj, proj)


def _s5_spread(rows, cols, row_period, col_period, row_div, col_div):
    r = lax.broadcasted_iota(jnp.int32, (rows, cols), 0)
    c = lax.broadcasted_iota(jnp.int32, (rows, cols), 1)
    same = jnp.logical_and(r // row_div == c // col_div, r % row_period == c % col_period)
    return same.astype(BF16)


def _s5_expand(compact, spread, rows_per_group, lanes_per_group):
    full = jnp.dot(compact, spread, preferred_element_type=F32)
    r = lax.broadcasted_iota(jnp.int32, full.shape, 0)
    c = lax.broadcasted_iota(jnp.int32, full.shape, 1)
    keep = (r // rows_per_group) % SSM_PACK == (c // lanes_per_group) % SSM_PACK
    return jnp.where(keep, full, 0.0).astype(BF16)


def _s5_kernel(u_ref, toep_ref, win_ref, vout_ref, ar_ref, ai_ref, y_ref, sin_ref, acc_ref):
    L, H, P = SSM_CHUNK, SSM_GROUP, SSM_STATE
    nc = u_ref.shape[0] // L
    jc = pl.program_id(1)
    per = acc_ref.shape[1] // HEAD_DIM
    xp = [jnp.concatenate([u_ref[pl.ds(2 * p, nc, stride=L), :], u_ref[pl.ds(2 * p + 1, nc, stride=L), :]],
                          axis=1).astype(BF16) for p in range(L // 2)]
    pair = lambda ref, p: ref[0, 0, 2 * p:2 * p + 2].reshape(2 * HEAD_DIM, ref.shape[-1])

    @pl.when(jc == 0)
    def _():
        nstate = sin_ref.shape[1]
        half = nstate // 2
        spread = _s5_spread(2 * P, nstate, P, P, P, half)
        b = sum(jnp.dot(xp[p], _s5_expand(pair(win_ref, p), spread, H, P), preferred_element_type=F32)
                for p in range(L // 2))
        er, ei = b[:, :half], b[:, half:]
        row = lax.broadcasted_iota(jnp.int32, er.shape, 0)
        for k in range(int(math.log2(nc))):
            sh = 1 << k
            sr = jnp.where(row >= sh, pltpu.roll(er, sh, axis=0), 0.0)
            si = jnp.where(row >= sh, pltpu.roll(ei, sh, axis=0), 0.0)
            ar, ai = ar_ref[0, k:k + 1, :], ai_ref[0, k:k + 1, :]
            er, ei = er + ar * sr - ai * si, ei + ar * si + ai * sr
        sin_ref[:, :half] = jnp.where(row >= 1, pltpu.roll(er, 1, axis=0), 0.0).astype(BF16)
        sin_ref[:, half:] = jnp.where(row >= 1, pltpu.roll(ei, 1, axis=0), 0.0).astype(BF16)

    spread = _s5_spread(per * H, per * HEAD_DIM, H, H, H, HEAD_DIM)
    acc_ref[...] = jnp.dot(sin_ref[...], _s5_expand(vout_ref[0, 0], spread, P, H),
                           preferred_element_type=F32)
    for p in range(L // 2):
        @pl.when(2 * p < (jc + 1) * per)
        def _(p=p):
            acc_ref[...] += jnp.dot(xp[p], _s5_expand(pair(toep_ref, p), spread, H, H),
                                    preferred_element_type=F32)
    for t in range(per):
        y_ref[pl.ds(jc * per + t, nc, stride=L), :] = acc_ref[:, t * HEAD_DIM:(t + 1) * HEAD_DIM]


def s5_tables(a_re, a_im, b_re, b_im, c_re, c_im, d_skip, log_dt, ncol):
    L, P, H, G, K8 = SSM_CHUNK, SSM_STATE, SSM_GROUP, SSM_GROUPS, SSM_PACK
    GC = G // K8
    per = L // ncol
    dt = jnp.exp(log_dt.astype(F32))[:, None]
    ar, ai = a_re.astype(F32), a_im.astype(F32)
    mag = jnp.exp(ar * dt)
    lr, li = mag * jnp.cos(ai * dt), mag * jnp.sin(ai * dt)
    den = ar * ar + ai * ai
    nr, ni = lr - 1.0, li
    cr = (nr * ar + ni * ai) / den
    ci = (ni * ar - nr * ai) / den
    br, bi = b_re.astype(F32), b_im.astype(F32)
    bbr = cr[..., None] * br - ci[..., None] * bi
    bbi = cr[..., None] * bi + ci[..., None] * br
    pr, pi = [jnp.ones_like(lr)], [jnp.zeros_like(li)]
    for _ in range(L):
        pr, pi = pr + [pr[-1] * lr - pi[-1] * li], pi + [pr[-1] * li + pi[-1] * lr]
    pr, pi = jnp.stack(pr, 1), jnp.stack(pi, 1)
    ccr, cci = c_re.astype(F32), c_im.astype(F32)
    clr = ccr[:, None] * pr[:, :L, None, :] - cci[:, None] * pi[:, :L, None, :]
    cli = ccr[:, None] * pi[:, :L, None, :] + cci[:, None] * pr[:, :L, None, :]
    kern = (jnp.einsum('gdnp,gph->gdnh', clr, bbr, precision='highest')
            - jnp.einsum('gdnp,gph->gdnh', cli, bbi, precision='highest'))
    kern = kern.at[:, 0].add(jnp.eye(H, dtype=F32)[None] * d_skip.astype(F32).reshape(G, H)[:, :, None])
    tau = jnp.arange(L)
    diff = tau[None, :] - tau[:, None]
    kt = kern[:, jnp.clip(diff, 0, L - 1)]
    kt = jnp.where((diff >= 0)[None, :, :, None, None], kt, 0.0)
    toep = kt.reshape(GC, K8, L, ncol, per, H, H).transpose(0, 3, 2, 1, 6, 4, 5)
    toep = toep.reshape(GC, ncol, L, K8 * H, per * H)
    qr, qi = pr[:, L - 1 - tau], pi[:, L - 1 - tau]
    bt_r, bt_i = bbr.transpose(0, 2, 1)[:, None], bbi.transpose(0, 2, 1)[:, None]
    wre = qr[:, :, None, :] * bt_r - qi[:, :, None, :] * bt_i
    wim = qr[:, :, None, :] * bt_i + qi[:, :, None, :] * bt_r
    win = jnp.concatenate([wre, wim], -1)
    win = win.reshape(GC, K8, L, H, 2 * P).transpose(0, 2, 1, 3, 4).reshape(GC, 1, L, K8 * H, 2 * P)
    pr1, pi1 = pr[:, 1:], pi[:, 1:]
    vre = ccr[:, None] * pr1[:, :, None, :] - cci[:, None] * pi1[:, :, None, :]
    vim = ccr[:, None] * pi1[:, :, None, :] + cci[:, None] * pr1[:, :, None, :]
    vout = jnp.stack([vre, -vim], 1)
    vout = vout.reshape(GC, K8, 2, ncol, per, H, P).transpose(0, 3, 2, 1, 6, 4, 5)
    vout = vout.reshape(GC, ncol, 2 * K8 * P, per * H)
    zr, zi = [pr[:, L]], [pi[:, L]]
    for _ in range(15):
        zr, zi = zr + [zr[-1] * zr[-1] - zi[-1] * zi[-1]], zi + [2.0 * zr[-1] * zi[-1]]
    pack = lambda z: jnp.stack(z, 1).reshape(GC, K8, 16, P).transpose(0, 2, 1, 3).reshape(GC, 16, K8 * P)
    return toep.astype(BF16), win.astype(BF16), vout.astype(BF16), pack(zr), pack(zi)


def s5_scan(u, s5_params, ncol=4):
    toep, win, vout, zr, zi = s5_tables(*s5_params, ncol=ncol)
    s, width = u.shape
    L, K8, P = SSM_CHUNK, SSM_PACK, SSM_STATE
    gc = width // HEAD_DIM
    cw = (L // ncol) * HEAD_DIM
    nstate = 2 * K8 * P
    return pl.pallas_call(
        _s5_kernel,
        grid=(gc, ncol),
        in_specs=[pl.BlockSpec((s, HEAD_DIM), lambda g, j: (0, g)),
                  pl.BlockSpec((1, 1) + toep.shape[2:], lambda g, j: (g, j, 0, 0, 0)),
                  pl.BlockSpec((1, 1) + win.shape[2:], lambda g, j: (g, 0, 0, 0, 0)),
                  pl.BlockSpec((1, 1) + vout.shape[2:], lambda g, j: (g, j, 0, 0)),
                  pl.BlockSpec((1, 16, nstate // 2), lambda g, j: (g, 0, 0)),
                  pl.BlockSpec((1, 16, nstate // 2), lambda g, j: (g, 0, 0))],
        out_specs=pl.BlockSpec((s, HEAD_DIM), lambda g, j: (0, g)),
        out_shape=jax.ShapeDtypeStruct((s, width), F32),
        scratch_shapes=[pltpu.VMEM((s // L, nstate), BF16), pltpu.VMEM((s // L, cw), F32)],
        compiler_params=_params("arbitrary", "arbitrary"),
        name="s5_scan",
    )(u, toep, win, vout, zr, zi)


def _glu_kernel(y_ref, yj_ref, w_ref, o_ref):
    yg = jax.nn.gelu(y_ref[...])
    z = jnp.dot(yg.astype(BF16), w_ref[...].astype(BF16), preferred_element_type=F32)
    o_ref[...] = (jax.nn.gelu(yj_ref[...]) * jax.nn.sigmoid(z)).astype(o_ref.dtype)


def gelu_glu(y, glu_w, lead, tm=512, tn=512):
    m, k = y.shape
    tm = min(tm, m)
    return pl.pallas_call(
        _glu_kernel,
        grid=(m // tm, k // tn),
        in_specs=[pl.BlockSpec((tm, k), lambda i, j: (i, 0)),
                  pl.BlockSpec((tm, tn), lambda i, j: (i, j)),
                  _wspec(lead, (k, tn), lambda i, j: (0, j))],
        out_specs=pl.BlockSpec((tm, tn), lambda i, j: (i, j)),
        out_shape=jax.ShapeDtypeStruct((m, k), BF16),
        compiler_params=_params("arbitrary", "arbitrary"),
        name="gelu_glu",
    )(y, y, glu_w)


def _hgrn_kernel(q_ref, f_ref, i_ref, g_ref, lb_ref, nw_ref, o_ref, state_ref):
    @pl.when(pl.program_id(1) == 0)
    def _():
        state_ref[...] = jnp.zeros_like(state_ref)

    tc, d = q_ref.shape
    sub = HGRN_SUB
    nblk = tc // sub
    lb = lb_ref[...]
    q = q_ref[...]
    v = i_ref[...]
    f = lb + (1.0 - lb) * jax.nn.sigmoid(f_ref[...])
    k = 1.0 - f
    r_i = lax.broadcasted_iota(jnp.int32, (tc, tc), 0)
    c_i = lax.broadcasted_iota(jnp.int32, (tc, tc), 1)
    tril = (c_i <= r_i).astype(BF16)
    gcum = sum(jnp.dot(tril, p, preferred_element_type=F32)
               for p in _split_bf16(jnp.log(f), parts=3))
    gtot = gcum[tc - 1:tc, :]
    state = state_ref[...]
    o = _nt((q * jnp.exp(gcum)).astype(BF16), state.astype(BF16))

    g3, q3, k3, v3 = (x.reshape(nblk, sub, d) for x in (gcum, q, k, v))
    s_idx = lax.broadcasted_iota(jnp.int32, (nblk, sub, d), 1)
    rows = []
    for t in range(sub):
        dec = jnp.exp(jnp.where(s_idx <= t, g3[:, t:t + 1] - g3, -jnp.inf))
        a = jnp.sum(q3[:, t:t + 1] * k3 * dec, axis=2, keepdims=True)
        rows.append(jnp.sum(a * v3, axis=1, keepdims=True))
    o = o + jnp.concatenate(rows, axis=1).reshape(tc, d)

    v16 = v.astype(BF16)
    b = sub
    while 2 * b <= tc:
        m = tc // (2 * b)
        g4, q4, k4 = (x.reshape(m, 2 * b, d) for x in (gcum, q, k))
        ref = g4[:, b:b + 1]
        qd = (q4[:, b:] * jnp.exp(g4[:, b:] - ref)).astype(BF16)
        kd = (k4[:, :b] * jnp.exp(ref - g4[:, :b])).astype(BF16)
        att = jnp.einsum('mqd,mkd->mqk', qd, kd, preferred_element_type=F32)
        ob = jnp.einsum('mqk,mkd->mqd', att.astype(BF16), v16.reshape(m, 2 * b, d)[:, :b],
                        preferred_element_type=F32)
        o = o + jnp.concatenate([jnp.zeros_like(ob), ob], axis=1).reshape(tc, d)
        b *= 2

    kdec = (k * jnp.exp(gtot - gcum)).astype(BF16)
    state_ref[...] = state * jnp.exp(gtot) + _tn(v16, kdec)
    on = o * lax.rsqrt(jnp.mean(o * o, axis=-1, keepdims=True) + RMS_EPS) * nw_ref[...]
    gate = g_ref[...]
    o_ref[...] = (on * (gate * jax.nn.sigmoid(gate))).astype(o_ref.dtype)


def hgrn2(proj, lb, norm_w, offs, tc=512):
    s = proj.shape[0]
    tc = min(tc, s)
    specs = [pl.BlockSpec((tc, HEAD_DIM), functools.partial(lambda h, i, o: (i, o + h), o=o)) for o in offs]
    return pl.pallas_call(
        _hgrn_kernel,
        grid=(N_HEADS, s // tc),
        in_specs=specs + [pl.BlockSpec((1, HEAD_DIM), lambda h, i: (0, h)),
                          pl.BlockSpec((1, HEAD_DIM), lambda h, i: (0, 0))],
        out_specs=pl.BlockSpec((tc, HEAD_DIM), lambda h, i: (i, h)),
        out_shape=jax.ShapeDtypeStruct((s, HALF_WIDTH), BF16),
        scratch_shapes=[pltpu.VMEM((HEAD_DIM, HEAD_DIM), F32)],
        compiler_params=_params("arbitrary", "arbitrary"),
        name="hgrn2",
    )(proj, proj, proj, proj, lb.reshape(1, HALF_WIDTH).astype(F32), norm_w.reshape(1, HEAD_DIM).astype(F32))


def rope_tables(s):
    half = ROPE_DIM // 2
    inv = ROPE_THETA ** (-jnp.arange(half, dtype=F32) / half)
    ang = jnp.arange(s).astype(F32)[:, None] * inv[None, :]
    cos, sin = jnp.cos(ang), jnp.sin(ang)
    one = jnp.ones((s, HEAD_DIM - ROPE_DIM), F32)
    zero = jnp.zeros((s, HEAD_DIM - ROPE_DIM), F32)
    z16 = jnp.zeros((s, half), F32)
    c = jnp.concatenate([cos, cos, one], 1)
    s_up = jnp.concatenate([z16, sin, zero], 1)
    s_dn = jnp.concatenate([-sin, z16, zero], 1)
    return c, s_up, s_dn


def _rope(x, c, s_up, s_dn):
    half = ROPE_DIM // 2
    return (x * c + pltpu.roll(x, half, axis=1) * s_up
            + pltpu.roll(x, HEAD_DIM - half, axis=1) * s_dn)


def _moba_prep_kernel(k_ref, v_ref, c_ref, su_ref, sd_ref, kr_ref, vt_ref, km_ref):
    c, su, sd = c_ref[...], su_ref[...], sd_ref[...]
    for h in range(N_HEADS):
        sl = slice(h * HEAD_DIM, (h + 1) * HEAD_DIM)
        kr = _rope(k_ref[:, sl], c, su, sd)
        kr_ref[:, sl] = kr.astype(BF16)
        km_ref[0, :, sl] = jnp.mean(kr, axis=0, keepdims=True)
    vt_ref[0] = v_ref[...].T.astype(BF16)


def moba_prep(proj, k_blk, v_blk, tables):
    s = proj.shape[0]
    nb = s // MOBA_BLOCK
    tab = pl.BlockSpec((MOBA_BLOCK, HEAD_DIM), lambda n: (n, 0))
    kr, vt, km = pl.pallas_call(
        _moba_prep_kernel,
        grid=(nb,),
        in_specs=[pl.BlockSpec((MOBA_BLOCK, HALF_WIDTH), lambda n: (n, k_blk)),
                  pl.BlockSpec((MOBA_BLOCK, HALF_WIDTH), lambda n: (n, v_blk)),
                  tab, tab, tab],
        out_specs=[pl.BlockSpec((MOBA_BLOCK, HALF_WIDTH), lambda n: (n, 0)),
                   pl.BlockSpec((1, HALF_WIDTH, MOBA_BLOCK), lambda n: (n, 0, 0)),
                   pl.BlockSpec((1, 1, HALF_WIDTH), lambda n: (n, 0, 0))],
        out_shape=[jax.ShapeDtypeStruct((s, HALF_WIDTH), BF16),
                   jax.ShapeDtypeStruct((nb, HALF_WIDTH, MOBA_BLOCK), BF16),
                   jax.ShapeDtypeStruct((nb, 1, HALF_WIDTH), F32)],
        compiler_params=_params("arbitrary"),
        name="moba_prep",
    )(proj, proj, *tables)
    return kr, vt, km.reshape(nb, HALF_WIDTH)


def _moba_kernel(q_ref, c_ref, su_ref, sd_ref, k_ref, vt_ref, km_ref, o_ref, sel_ref, s_ref, *, unroll):
    own = pl.program_id(1)
    blk, d = MOBA_BLOCK, HEAD_DIM
    nb = km_ref.shape[0]
    heads = q_ref.shape[1] // d
    cols = lambda hh: slice(hh * d, (hh + 1) * d)
    c, su, sd = c_ref[...], su_ref[...], sd_ref[...]
    o0 = pl.multiple_of(own * blk, blk)

    qs, init = [], []
    for hh in range(heads):
        qf = _rope(q_ref[:, cols(hh)], c, su, sd)
        q = (qf * d ** -0.5).astype(BF16)
        qs.append(q)
        gate = _nt(km_ref[:, cols(hh)], qf, precision=lax.Precision.HIGHEST)
        row = lax.broadcasted_iota(jnp.int32, gate.shape, 0)
        g = jnp.where(row < own, gate, NEG)
        sel = jnp.zeros(gate.shape, F32)
        for j in range(MOBA_TOPK):
            m = jnp.max(g, axis=0, keepdims=True)
            idx = jnp.min(jnp.where(g == m, row, nb), axis=0, keepdims=True)
            pick = row == idx
            sel = jnp.where(pick, jnp.where(j < own, 1.0, 0.0), sel)
            g = jnp.where(pick, -jnp.inf, g)
        sel_ref[hh] = sel
        s0 = _nt(k_ref[pl.ds(o0, blk), cols(hh)], q)
        kpos = lax.broadcasted_iota(jnp.int32, s0.shape, 0)
        qpos = lax.broadcasted_iota(jnp.int32, s0.shape, 1)
        s0 = jnp.where(kpos <= qpos, s0, NEG)
        m0 = jnp.max(s0, axis=0, keepdims=True)
        p0 = jnp.exp(s0 - m0)
        l0 = jnp.sum(p0, axis=0, keepdims=True)
        acc0 = jnp.dot(vt_ref[own, cols(hh), :], p0.astype(BF16), preferred_element_type=F32)
        init.append((m0, l0, acc0))

    def raw_scores(it):
        for hh in range(heads):
            for u in range(unroll):
                n = jnp.minimum(it * unroll + u, nb - 1)
                start = pl.multiple_of(n * blk, blk)
                s_ref[it % 2, hh, u] = _nt(k_ref[pl.ds(start, blk), cols(hh)], qs[hh])

    def body(it, carry):
        out = []
        for hh in range(heads):
            m, l, acc = carry[hh]
            scores = [jnp.where(sel_ref[hh, pl.ds(it * unroll + u, 1), :] > 0.0,
                                s_ref[it % 2, hh, u], NEG) for u in range(unroll)]
            m_new = functools.reduce(jnp.maximum, [jnp.max(sn, axis=0, keepdims=True) for sn in scores], m)
            alpha = jnp.exp(m - m_new)
            l, acc = alpha * l, alpha * acc
            for u, sn in enumerate(scores):
                p = jnp.exp(sn - m_new)
                l = l + jnp.sum(p, axis=0, keepdims=True)
                acc = acc + jnp.dot(vt_ref[it * unroll + u, cols(hh), :], p.astype(BF16),
                                    preferred_element_type=F32)
            out.append((m_new, l, acc))
        raw_scores(it + 1)
        return tuple(out)

    trips = (own + unroll - 1) // unroll
    raw_scores(0)
    final = lax.fori_loop(0, trips, body, tuple(init))
    for hh in range(heads):
        _, l, acc = final[hh]
        o_ref[:, cols(hh)] = (acc / l).T.astype(o_ref.dtype)


def moba(proj, q_off, kr, vt, kmean, tables, unroll=4, heads=2):
    s = proj.shape[0]
    nb = s // MOBA_BLOCK
    assert nb % unroll == 0 and q_off % heads == 0 and N_HEADS % heads == 0
    w = heads * HEAD_DIM
    tab = pl.BlockSpec((MOBA_BLOCK, HEAD_DIM), lambda h, i: (i, 0))
    return pl.pallas_call(
        functools.partial(_moba_kernel, unroll=unroll),
        grid=(N_HEADS // heads, nb),
        in_specs=[pl.BlockSpec((MOBA_BLOCK, w), lambda h, i: (i, q_off // heads + h)),
                  tab, tab, tab,
                  pl.BlockSpec((s, w), lambda h, i: (0, h)),
                  pl.BlockSpec((nb, w, MOBA_BLOCK), lambda h, i: (0, h, 0)),
                  pl.BlockSpec((nb, w), lambda h, i: (0, h))],
        out_specs=pl.BlockSpec((MOBA_BLOCK, w), lambda h, i: (i, h)),
        out_shape=jax.ShapeDtypeStruct((s, HALF_WIDTH), BF16),
        scratch_shapes=[pltpu.VMEM((heads, nb, MOBA_BLOCK), F32),
                        pltpu.VMEM((2, heads, unroll, MOBA_BLOCK, MOBA_BLOCK), F32)],
        compiler_params=_params("arbitrary", "arbitrary"),
        name="moba",
    )(proj, *tables, kr, vt, kmean)


def _even_mixer(h, w_in, w_out, e, s5_params, glu_w):
    qkv = matmul(h, w_in, (e,), BF16, col0=0, ncols=3 * HALF_WIDTH)
    u = matmul(h, w_in, (e,), F32, col0=3 * HALF_WIDTH, ncols=HALF_WIDTH)
    o_a = stick_breaking(qkv, 0, N_HEADS, 2 * N_HEADS)
    o_b = gelu_glu(s5_scan(u, s5_params), glu_w, (e,))
    return matmul_cat([o_a, o_b], w_out, (e,), BF16, tn=512)


def _odd_mixer(h, w_in, w_out, o, lb, norm_w):
    s = h.shape[0]
    proj = matmul(h, w_in, (o,), F32)
    o_c = hgrn2(proj, lb, norm_w, (0, N_HEADS, 2 * N_HEADS, 3 * N_HEADS))
    tables = rope_tables(s)
    kr, vt, kmean = moba_prep(proj, 5, 6, tables)
    o_d = moba(proj, 4 * N_HEADS, kr, vt, kmean, tables)
    return matmul_cat([o_c, o_d], w_out, (o,), BF16, tn=512)


def kernel(x, c, ada_w, ada_table, norm_pre, norm_post, ffn_w1, ffn_w3, ffn_w2, ev_w_in, ev_w_out,
           s5_a_re, s5_a_im, s5_b_re, s5_b_im, s5_c_re, s5_c_im, s5_d, s5_log_dt, s5_glu_w,
           od_w_in, od_w_out, hgrn_lb, hgrn_norm_w):
    bsz, seq, d = x.shape
    depth = ada_table.shape[0]
    mod_shared = ada_project(c, ada_w).reshape(bsz, 9, d)
    lb_cum = jnp.cumsum(jax.nn.softmax(hgrn_lb.astype(F32), axis=0), axis=0)
    lb_all = lb_cum - lb_cum[:1]

    outs = []
    for b in range(bsz):
        xb = x[b]
        mods = [mod_shared[b] + ada_table[layer] for layer in range(depth)]
        subs = [(layer, slot) for layer in range(depth) for slot in range(3)]
        first = subs[0]
        h = pre_norm(xb, norm_pre[first[0], first[1]], mods[first[0]][1], mods[first[0]][0])
        for n, (layer, slot) in enumerate(subs):
            mod = mods[layer]
            if slot == 1:
                if layer % 2 == 0:
                    e = layer // 2
                    y = _even_mixer(h, ev_w_in, ev_w_out, e,
                                    (s5_a_re[e], s5_a_im[e], s5_b_re[e], s5_b_im[e], s5_c_re[e], s5_c_im[e],
                                     s5_d[e], s5_log_dt[e]), s5_glu_w)
                else:
                    o = layer // 2
                    y = _odd_mixer(h, od_w_in, od_w_out, o, lb_all[layer], hgrn_norm_w[o])
                res_w = 1.0
            else:
                f = slot // 2
                g = gate_up(h, ffn_w1, ffn_w3, (layer, f))
                y = matmul_cat([g], ffn_w2, (layer, f), BF16)
                res_w = 0.5
            nxt = None
            if n + 1 < len(subs):
                nl, ns = subs[n + 1]
                nxt = (norm_pre[nl, ns], mods[nl][3 * ns + 1], mods[nl][3 * ns])
            xb, h = post_norm(y, xb, norm_post[layer, slot], mod[3 * slot + 2], res_w, nxt)
        outs.append(xb)
    return jnp.stack(outs, axis=0)
```

```python
import functools
import math

import jax
import jax.numpy as jnp
from jax import lax
from jax.experimental import pallas as pl
from jax.experimental.pallas import tpu as pltpu

F32 = jnp.float32
BF16 = jnp.bfloat16

HEAD_DIM = 128
N_HEADS = 16
HALF_WIDTH = N_HEADS * HEAD_DIM
SSM_GROUP = 16
SSM_GROUPS = HALF_WIDTH // SSM_GROUP
SSM_STATE = 64
SSM_CHUNK = 16
SSM_PACK = HEAD_DIM // SSM_GROUP
HGRN_SUB = 16
SB_BLOCK = 256
MOBA_BLOCK = 256
MOBA_TOPK = 3
ROPE_THETA = 500000.0
ROPE_DIM = HEAD_DIM // 4
RMS_EPS = 1e-6
NEG = -1e30
EXP_UNDERFLOW = -104.0

VMEM_LIMIT = 56 * 1024 * 1024


def _params(*sem):
    return pltpu.CompilerParams(dimension_semantics=sem, vmem_limit_bytes=VMEM_LIMIT)


def _nt(a, b, **kw):
    return lax.dot_general(a, b, (((1,), (1,)), ((), ())), preferred_element_type=F32, **kw)


def _tn(a, b):
    return lax.dot_general(a, b, (((0,), (0,)), ((), ())), preferred_element_type=F32)


def _split_bf16(x, parts=2):
    out = []
    for _ in range(parts):
        p = x.astype(BF16)
        out.append(p)
        x = x - p.astype(F32)
    return out


def _wspec(lead, block, index_map):
    lead = tuple(lead)
    return pl.BlockSpec((None,) * len(lead) + tuple(block), lambda *g: lead + tuple(index_map(*g)))


def _ada_kernel(c_ref, w_ref, o_ref):
    c = c_ref[...]
    a = c * jax.nn.sigmoid(c)
    o_ref[...] = jnp.dot(a, w_ref[...], precision=lax.Precision.HIGHEST,
                         preferred_element_type=F32)


def ada_project(c, ada_w, tn=1024):
    d, n = ada_w.shape
    bsz = c.shape[0]
    assert bsz <= 8
    c8 = jnp.zeros((8, d), F32).at[:bsz].set(c.astype(F32))
    out = pl.pallas_call(
        _ada_kernel,
        grid=(n // tn,),
        in_specs=[pl.BlockSpec((8, d), lambda j: (0, 0)),
                  pl.BlockSpec((d, tn), lambda j: (0, j))],
        out_specs=pl.BlockSpec((8, tn), lambda j: (0, j)),
        out_shape=jax.ShapeDtypeStruct((8, n), F32),
        compiler_params=_params("arbitrary"),
        name="ada_project",
    )(c8, ada_w)
    return out[:bsz]


def _rms(x, gain):
    return x * lax.rsqrt(jnp.mean(x * x, axis=-1, keepdims=True) + RMS_EPS) * gain


def _pre_kernel(x_ref, vec_ref, h_ref):
    h = _rms(x_ref[...], vec_ref[0:1, :]) * (1.0 + vec_ref[1:2, :]) + vec_ref[2:3, :]
    h_ref[...] = h.astype(h_ref.dtype)


def pre_norm(x, g_pre, scale, shift, tm=256):
    m, d = x.shape
    vec = jnp.zeros((8, d), F32).at[0].set(g_pre).at[1].set(scale).at[2].set(shift)
    return pl.pallas_call(
        _pre_kernel,
        grid=(m // tm,),
        in_specs=[pl.BlockSpec((tm, d), lambda i: (i, 0)),
                  pl.BlockSpec((8, d), lambda i: (0, 0))],
        out_specs=pl.BlockSpec((tm, d), lambda i: (i, 0)),
        out_shape=jax.ShapeDtypeStruct((m, d), BF16),
        compiler_params=_params("arbitrary"),
        name="pre_norm",
    )(x, vec)


def _post_kernel(y_ref, x_ref, vec_ref, xo_ref, *maybe_h_ref, res_w):
    yn = _rms(y_ref[...].astype(F32), vec_ref[0:1, :])
    xn = x_ref[...] + (res_w * vec_ref[1:2, :]) * yn
    xo_ref[...] = xn
    if maybe_h_ref:
        h = _rms(xn, vec_ref[2:3, :]) * (1.0 + vec_ref[3:4, :]) + vec_ref[4:5, :]
        maybe_h_ref[0][...] = h.astype(BF16)


def post_norm(y, x, g_post, gate, res_w, nxt=None, tm=256):
    m, d = x.shape
    vec = jnp.zeros((8, d), F32).at[0].set(g_post).at[1].set(gate)
    out_shape = [jax.ShapeDtypeStruct((m, d), F32)]
    out_specs = [pl.BlockSpec((tm, d), lambda i: (i, 0))]
    if nxt is not None:
        vec = vec.at[2].set(nxt[0]).at[3].set(nxt[1]).at[4].set(nxt[2])
        out_shape.append(jax.ShapeDtypeStruct((m, d), BF16))
        out_specs.append(pl.BlockSpec((tm, d), lambda i: (i, 0)))
    res = pl.pallas_call(
        functools.partial(_post_kernel, res_w=res_w),
        grid=(m // tm,),
        in_specs=[pl.BlockSpec((tm, d), lambda i: (i, 0)),
                  pl.BlockSpec((tm, d), lambda i: (i, 0)),
                  pl.BlockSpec((8, d), lambda i: (0, 0))],
        out_specs=out_specs,
        out_shape=out_shape,
        compiler_params=_params("arbitrary"),
        name="post_norm",
    )(y, x, vec)
    return (res[0], res[1]) if nxt is not None else (res[0], None)


def _mm_kernel(a_ref, w_ref, o_ref, *, tc):
    a = a_ref[...]
    for n0 in range(0, o_ref.shape[1], tc):
        o_ref[:, n0:n0 + tc] = jnp.dot(a, w_ref[:, n0:n0 + tc].astype(BF16),
                                       preferred_element_type=F32).astype(o_ref.dtype)


def matmul(a, w, lead, out_dtype, col0=0, ncols=None, tm=2048, tn=512):
    m, k = a.shape
    n = w.shape[-1] - col0 if ncols is None else ncols
    tm, tn = min(tm, m), min(tn, n)
    assert col0 % tn == 0 and n % tn == 0 and m % tm == 0
    j0 = col0 // tn
    return pl.pallas_call(
        functools.partial(_mm_kernel, tc=min(256, tn)),
        grid=(m // tm, n // tn),
        in_specs=[pl.BlockSpec((tm, k), lambda i, j: (i, 0), pipeline_mode=pl.Buffered(1)),
                  _wspec(lead, (k, tn), lambda i, j: (0, j + j0))],
        out_specs=pl.BlockSpec((tm, tn), lambda i, j: (i, j)),
        out_shape=jax.ShapeDtypeStruct((m, n), out_dtype),
        compiler_params=_params("arbitrary", "arbitrary"),
        name="matmul",
    )(a, w)


def _mm_cat_kernel(*refs, chunks):
    a_refs, w_ref, o_ref = refs[:len(chunks)], refs[len(chunks)], refs[len(chunks) + 1]
    acc = None
    row = 0
    for a_ref, kc in zip(a_refs, chunks):
        for k0 in range(0, a_ref.shape[1], kc):
            part = jnp.dot(a_ref[:, k0:k0 + kc], w_ref[row + k0:row + k0 + kc, :].astype(BF16),
                           preferred_element_type=F32)
            acc = part if acc is None else acc + part
        row += a_ref.shape[1]
    o_ref[...] = acc.astype(o_ref.dtype)


def _k_chunk(k, limit=6144):
    best = HEAD_DIM
    for c in range(HEAD_DIM, min(k, limit) + 1, HEAD_DIM):
        if k % c == 0:
            best = c
    return best


def matmul_cat(a_list, w, lead, out_dtype, tm=1024, tn=256):
    m = a_list[0].shape[0]
    k_total, n = w.shape[-2], w.shape[-1]
    assert sum(a.shape[1] for a in a_list) == k_total and m % tm == 0 and n % tn == 0
    chunks = tuple(_k_chunk(a.shape[1]) for a in a_list)
    a_specs = [pl.BlockSpec((tm, a.shape[1]), lambda i, j: (i, 0), pipeline_mode=pl.Buffered(1))
               for a in a_list]
    return pl.pallas_call(
        functools.partial(_mm_cat_kernel, chunks=chunks),
        grid=(m // tm, n // tn),
        in_specs=a_specs + [_wspec(lead, (k_total, tn), lambda i, j: (0, j))],
        out_specs=pl.BlockSpec((tm, tn), lambda i, j: (i, j)),
        out_shape=jax.ShapeDtypeStruct((m, n), out_dtype),
        compiler_params=_params("arbitrary", "arbitrary"),
        name="matmul_cat",
    )(*a_list, w)


def _gateup_kernel(a_ref, w1_ref, w3_ref, o_ref, *, tc):
    a = a_ref[...]
    for n0 in range(0, o_ref.shape[1], tc):
        g = jnp.dot(a, w1_ref[:, n0:n0 + tc].astype(BF16), preferred_element_type=F32)
        u = jnp.dot(a, w3_ref[:, n0:n0 + tc].astype(BF16), preferred_element_type=F32)
        o_ref[:, n0:n0 + tc] = (g * jax.nn.sigmoid(g) * u).astype(o_ref.dtype)


def gate_up(a, w1, w3, lead, tm=1024, tn=512):
    m, k = a.shape
    n = w1.shape[-1]
    tm, tn = min(tm, m), min(tn, n)
    wspec = _wspec(lead, (k, tn), lambda i, j: (0, j))
    return pl.pallas_call(
        functools.partial(_gateup_kernel, tc=min(256, tn)),
        grid=(m // tm, pl.cdiv(n, tn)),
        in_specs=[pl.BlockSpec((tm, k), lambda i, j: (i, 0), pipeline_mode=pl.Buffered(1)), wspec, wspec],
        out_specs=pl.BlockSpec((tm, tn), lambda i, j: (i, j)),
        out_shape=jax.ShapeDtypeStruct((m, n), BF16),
        compiler_params=_params("arbitrary", "arbitrary"),
        name="gate_up",
    )(a, w1, w3)


def _sb_kernel(q_ref, k_ref, v_ref, o_ref, *, t):
    i = pl.program_id(1)
    nch = q_ref.shape[0] // t
    qs = [q_ref[ch * t:(ch + 1) * t, :] for ch in range(nch)]
    row = lax.broadcasted_iota(jnp.int32, (t, t), 0)
    col = lax.broadcasted_iota(jnp.int32, (t, t), 1)
    suffix = (row > col).astype(BF16)
    scale = HEAD_DIM ** -0.5

    def cond(carry):
        step, cs, _ = carry
        top = functools.reduce(jnp.maximum, [jnp.max(c) for c in cs])
        return jnp.logical_and(nch * i + nch - 1 - step >= 0, top >= EXP_UNDERFLOW)

    def body(carry):
        step, cs, accs = carry
        new_c, new_acc = [], []
        for ch in range(nch):
            qblk = nch * i + ch
            j = qblk - step
            jc = jnp.maximum(j, 0)
            start = pl.multiple_of(jc * t, t)
            kj = k_ref[pl.ds(start, t), :]
            vj = v_ref[pl.ds(start, t), :]
            z = _nt(qs[ch], kj) * scale
            mask = (col - row) < jnp.where(j >= 0, (qblk - jc) * t, -2 * t)
            ls_pos = jnp.minimum(z, 0.0) - jnp.log(1.0 + jnp.exp(-jnp.abs(z)))
            lk = jnp.where(mask, ls_pos - z, 0.0)
            later = sum(jnp.dot(p, suffix, preferred_element_type=F32) for p in _split_bf16(lk))
            w = jnp.where(mask, jnp.exp(ls_pos + later + cs[ch]), 0.0)
            new_acc.append(accs[ch] + jnp.dot(w.astype(BF16), vj, preferred_element_type=F32))
            new_c.append(cs[ch] + jnp.sum(lk, axis=1, keepdims=True))
        return step + 1, tuple(new_c), tuple(new_acc)

    init = (jnp.int32(0), tuple(jnp.zeros((t, 1), F32) for _ in range(nch)),
            tuple(jnp.zeros((t, HEAD_DIM), F32) for _ in range(nch)))
    _, _, accs = lax.while_loop(cond, body, init)
    for ch in range(nch):
        o_ref[ch * t:(ch + 1) * t, :] = accs[ch].astype(o_ref.dtype)


def stick_breaking(proj, q_off, k_off, v_off, chains=2):
    s = proj.shape[0]
    t = min(SB_BLOCK, s)
    tq = min(chains * t, s)
    return pl.pallas_call(
        functools.partial(_sb_kernel, t=t),
        grid=(N_HEADS, s // tq),
        in_specs=[pl.BlockSpec((tq, HEAD_DIM), lambda h, i: (i, q_off + h)),
                  pl.BlockSpec((s, HEAD_DIM), lambda h, i: (0, k_off + h)),
                  pl.BlockSpec((s, HEAD_DIM), lambda h, i: (0, v_off + h))],
        out_specs=pl.BlockSpec((tq, HEAD_DIM), lambda h, i: (i, h)),
        out_shape=jax.ShapeDtypeStruct((s, HALF_WIDTH), BF16),
        compiler_params=_params("arbitrary", "arbitrary"),
        name="stick_breaking",
    )(proj, proj, proj)


def _s5_spread(rows, cols, row_period, col_period, row_div, col_div):
    r = lax.broadcasted_iota(jnp.int32, (rows, cols), 0)
    c = lax.broadcasted_iota(jnp.int32, (rows, cols), 1)
    same = jnp.logical_and(r // row_div == c // col_div, r % row_period == c % col_period)
    return same.astype(BF16)


def _s5_own_group(shape, rows_per_group, lanes_per_group):
    r = lax.broadcasted_iota(jnp.int32, shape, 0)
    c = lax.broadcasted_iota(jnp.int32, shape, 1)
    return (r // rows_per_group) % SSM_PACK == (c // lanes_per_group) % SSM_PACK


def _s5_expand(compact, spread, keep):
    full = jnp.dot(compact, spread, preferred_element_type=F32)
    return jnp.where(keep, full, 0.0).astype(BF16)


def _s5_kernel(u_ref, toep_ref, win_ref, vout_ref, ar_ref, ai_ref, y_ref, sin_ref, acc_ref):
    L, H, P = SSM_CHUNK, SSM_GROUP, SSM_STATE
    nc = u_ref.shape[0] // L
    jc = pl.program_id(1)
    per = acc_ref.shape[1] // HEAD_DIM
    xp = [jnp.concatenate([u_ref[pl.ds(2 * p, nc, stride=L), :], u_ref[pl.ds(2 * p + 1, nc, stride=L), :]],
                          axis=1).astype(BF16) for p in range(L // 2)]
    pair = lambda ref, p: ref[0, 0, 2 * p:2 * p + 2].reshape(2 * HEAD_DIM, ref.shape[-1])

    @pl.when(jc == 0)
    def _():
        nstate = sin_ref.shape[1]
        half = nstate // 2
        spread = _s5_spread(2 * P, nstate, P, P, P, half)
        keep = _s5_own_group((2 * HEAD_DIM, nstate), H, P)
        b = sum(jnp.dot(xp[p], _s5_expand(pair(win_ref, p), spread, keep), preferred_element_type=F32)
                for p in range(L // 2))
        er, ei = b[:, :half], b[:, half:]
        row = lax.broadcasted_iota(jnp.int32, er.shape, 0)
        for k in range(int(math.log2(nc))):
            sh = 1 << k
            sr = jnp.where(row >= sh, pltpu.roll(er, sh, axis=0), 0.0)
            si = jnp.where(row >= sh, pltpu.roll(ei, sh, axis=0), 0.0)
            ar, ai = ar_ref[0, k:k + 1, :], ai_ref[0, k:k + 1, :]
            er, ei = er + ar * sr - ai * si, ei + ar * si + ai * sr
        sin_ref[:, :half] = jnp.where(row >= 1, pltpu.roll(er, 1, axis=0), 0.0).astype(BF16)
        sin_ref[:, half:] = jnp.where(row >= 1, pltpu.roll(ei, 1, axis=0), 0.0).astype(BF16)

    spread = _s5_spread(per * H, per * HEAD_DIM, H, H, H, HEAD_DIM)
    keep_state = _s5_own_group((sin_ref.shape[1], per * HEAD_DIM), P, H)
    acc_ref[...] = jnp.dot(sin_ref[...], _s5_expand(vout_ref[0, 0], spread, keep_state),
                           preferred_element_type=F32)
    keep_in = _s5_own_group((2 * HEAD_DIM, per * HEAD_DIM), H, H)
    for p in range(L // 2):
        @pl.when(2 * p < (jc + 1) * per)
        def _(p=p):
            acc_ref[...] += jnp.dot(xp[p], _s5_expand(pair(toep_ref, p), spread, keep_in),
                                    preferred_element_type=F32)
    for t in range(per):
        y_ref[pl.ds(jc * per + t, nc, stride=L), :] = acc_ref[:, t * HEAD_DIM:(t + 1) * HEAD_DIM]


def s5_tables(a_re, a_im, b_re, b_im, c_re, c_im, d_skip, log_dt, ncol):
    L, P, H, G, K8 = SSM_CHUNK, SSM_STATE, SSM_GROUP, SSM_GROUPS, SSM_PACK
    GC = G // K8
    per = L // ncol
    dt = jnp.exp(log_dt.astype(F32))[:, None]
    ar, ai = a_re.astype(F32), a_im.astype(F32)
    mag = jnp.exp(ar * dt)
    lr, li = mag * jnp.cos(ai * dt), mag * jnp.sin(ai * dt)
    den = ar * ar + ai * ai
    nr, ni = lr - 1.0, li
    cr = (nr * ar + ni * ai) / den
    ci = (ni * ar - nr * ai) / den
    br, bi = b_re.astype(F32), b_im.astype(F32)
    bbr = cr[..., None] * br - ci[..., None] * bi
    bbi = cr[..., None] * bi + ci[..., None] * br
    pr, pi = [jnp.ones_like(lr)], [jnp.zeros_like(li)]
    for _ in range(L):
        pr, pi = pr + [pr[-1] * lr - pi[-1] * li], pi + [pr[-1] * li + pi[-1] * lr]
    pr, pi = jnp.stack(pr, 1), jnp.stack(pi, 1)
    ccr, cci = c_re.astype(F32), c_im.astype(F32)
    clr = ccr[:, None] * pr[:, :L, None, :] - cci[:, None] * pi[:, :L, None, :]
    cli = ccr[:, None] * pi[:, :L, None, :] + cci[:, None] * pr[:, :L, None, :]
    kern = (jnp.einsum('gdnp,gph->gdnh', clr, bbr, precision='highest')
            - jnp.einsum('gdnp,gph->gdnh', cli, bbi, precision='highest'))
    kern = kern.at[:, 0].add(jnp.eye(H, dtype=F32)[None] * d_skip.astype(F32).reshape(G, H)[:, :, None])
    tau = jnp.arange(L)
    diff = tau[None, :] - tau[:, None]
    kt = kern[:, jnp.clip(diff, 0, L - 1)]
    kt = jnp.where((diff >= 0)[None, :, :, None, None], kt, 0.0)
    toep = kt.reshape(GC, K8, L, ncol, per, H, H).transpose(0, 3, 2, 1, 6, 4, 5)
    toep = toep.reshape(GC, ncol, L, K8 * H, per * H)
    qr, qi = pr[:, L - 1 - tau], pi[:, L - 1 - tau]
    bt_r, bt_i = bbr.transpose(0, 2, 1)[:, None], bbi.transpose(0, 2, 1)[:, None]
    wre = qr[:, :, None, :] * bt_r - qi[:, :, None, :] * bt_i
    wim = qr[:, :, None, :] * bt_i + qi[:, :, None, :] * bt_r
    win = jnp.concatenate([wre, wim], -1)
    win = win.reshape(GC, K8, L, H, 2 * P).transpose(0, 2, 1, 3, 4).reshape(GC, 1, L, K8 * H, 2 * P)
    pr1, pi1 = pr[:, 1:], pi[:, 1:]
    vre = ccr[:, None] * pr1[:, :, None, :] - cci[:, None] * pi1[:, :, None, :]
    vim = ccr[:, None] * pi1[:, :, None, :] + cci[:, None] * pr1[:, :, None, :]
    vout = jnp.stack([vre, -vim], 1)
    vout = vout.reshape(GC, K8, 2, ncol, per, H, P).transpose(0, 3, 2, 1, 6, 4, 5)
    vout = vout.reshape(GC, ncol, 2 * K8 * P, per * H)
    zr, zi = [pr[:, L]], [pi[:, L]]
    for _ in range(15):
        zr, zi = zr + [zr[-1] * zr[-1] - zi[-1] * zi[-1]], zi + [2.0 * zr[-1] * zi[-1]]
    pack = lambda z: jnp.stack(z, 1).reshape(GC, K8, 16, P).transpose(0, 2, 1, 3).reshape(GC, 16, K8 * P)
    return toep.astype(BF16), win.astype(BF16), vout.astype(BF16), pack(zr), pack(zi)


def s5_scan(u, s5_params, ncol=4):
    toep, win, vout, zr, zi = s5_tables(*s5_params, ncol=ncol)
    s, width = u.shape
    L, K8, P = SSM_CHUNK, SSM_PACK, SSM_STATE
    gc = width // HEAD_DIM
    cw = (L // ncol) * HEAD_DIM
    nstate = 2 * K8 * P
    return pl.pallas_call(
        _s5_kernel,
        grid=(gc, ncol),
        in_specs=[pl.BlockSpec((s, HEAD_DIM), lambda g, j: (0, g)),
                  pl.BlockSpec((1, 1) + toep.shape[2:], lambda g, j: (g, j, 0, 0, 0)),
                  pl.BlockSpec((1, 1) + win.shape[2:], lambda g, j: (g, 0, 0, 0, 0)),
                  pl.BlockSpec((1, 1) + vout.shape[2:], lambda g, j: (g, j, 0, 0)),
                  pl.BlockSpec((1, 16, nstate // 2), lambda g, j: (g, 0, 0)),
                  pl.BlockSpec((1, 16, nstate // 2), lambda g, j: (g, 0, 0))],
        out_specs=pl.BlockSpec((s, HEAD_DIM), lambda g, j: (0, g)),
        out_shape=jax.ShapeDtypeStruct((s, width), F32),
        scratch_shapes=[pltpu.VMEM((s // L, nstate), BF16), pltpu.VMEM((s // L, cw), F32)],
        compiler_params=_params("arbitrary", "arbitrary"),
        name="s5_scan",
    )(u, toep, win, vout, zr, zi)


def _glu_kernel(y_ref, yj_ref, w_ref, o_ref):
    yg = jax.nn.gelu(y_ref[...])
    z = jnp.dot(yg.astype(BF16), w_ref[...].astype(BF16), preferred_element_type=F32)
    o_ref[...] = (jax.nn.gelu(yj_ref[...]) * jax.nn.sigmoid(z)).astype(o_ref.dtype)


def gelu_glu(y, glu_w, lead, tm=512, tn=512):
    m, k = y.shape
    tm = min(tm, m)
    return pl.pallas_call(
        _glu_kernel,
        grid=(m // tm, k // tn),
        in_specs=[pl.BlockSpec((tm, k), lambda i, j: (i, 0)),
                  pl.BlockSpec((tm, tn), lambda i, j: (i, j)),
                  _wspec(lead, (k, tn), lambda i, j: (0, j))],
        out_specs=pl.BlockSpec((tm, tn), lambda i, j: (i, j)),
        out_shape=jax.ShapeDtypeStruct((m, k), BF16),
        compiler_params=_params("arbitrary", "arbitrary"),
        name="gelu_glu",
    )(y, y, glu_w)


def _hgrn_head(q, f_logit, v, gate, lb, nw, tril, state):
    tc, d = q.shape
    sub = HGRN_SUB
    nblk = tc // sub
    f = lb + (1.0 - lb) * jax.nn.sigmoid(f_logit)
    k = 1.0 - f
    gcum = sum(jnp.dot(tril, p, preferred_element_type=F32)
               for p in _split_bf16(jnp.log(f), parts=3))
    gtot = gcum[tc - 1:tc, :]
    o = _nt((q * jnp.exp(gcum)).astype(BF16), state.astype(BF16))

    g3, q3, k3, v3 = (x.reshape(nblk, sub, d) for x in (gcum, q, k, v))
    s_idx = lax.broadcasted_iota(jnp.int32, (nblk, sub, d), 1)
    rows = []
    for t in range(sub):
        dec = jnp.exp(jnp.where(s_idx <= t, g3[:, t:t + 1] - g3, -jnp.inf))
        a = jnp.sum(q3[:, t:t + 1] * k3 * dec, axis=2, keepdims=True)
        rows.append(jnp.sum(a * v3, axis=1, keepdims=True))
    o = o + jnp.concatenate(rows, axis=1).reshape(tc, d)

    v16 = v.astype(BF16)
    b = sub
    while 2 * b <= tc:
        m = tc // (2 * b)
        g4, q4, k4 = (x.reshape(m, 2 * b, d) for x in (gcum, q, k))
        ref = g4[:, b:b + 1]
        qd = (q4[:, b:] * jnp.exp(g4[:, b:] - ref)).astype(BF16)
        kd = (k4[:, :b] * jnp.exp(ref - g4[:, :b])).astype(BF16)
        att = jnp.einsum('mqd,mkd->mqk', qd, kd, preferred_element_type=F32)
        ob = jnp.einsum('mqk,mkd->mqd', att.astype(BF16), v16.reshape(m, 2 * b, d)[:, :b],
                        preferred_element_type=F32)
        o = o + jnp.concatenate([jnp.zeros_like(ob), ob], axis=1).reshape(tc, d)
        b *= 2

    kdec = (k * jnp.exp(gtot - gcum)).astype(BF16)
    new_state = state * jnp.exp(gtot) + _tn(v16, kdec)
    on = o * lax.rsqrt(jnp.mean(o * o, axis=-1, keepdims=True) + RMS_EPS) * nw
    return (on * (gate * jax.nn.sigmoid(gate))).astype(BF16), new_state


def _hgrn_kernel(q_ref, f_ref, i_ref, g_ref, lb_ref, nw_ref, o_ref, state_ref):
    @pl.when(pl.program_id(1) == 0)
    def _():
        state_ref[...] = jnp.zeros_like(state_ref)

    tc = q_ref.shape[0]
    d = HEAD_DIM
    r_i = lax.broadcasted_iota(jnp.int32, (tc, tc), 0)
    c_i = lax.broadcasted_iota(jnp.int32, (tc, tc), 1)
    tril = (c_i <= r_i).astype(BF16)
    nw = nw_ref[...]
    for hh in range(q_ref.shape[1] // d):
        sl = slice(hh * d, (hh + 1) * d)
        out, new_state = _hgrn_head(q_ref[:, sl], f_ref[:, sl], i_ref[:, sl], g_ref[:, sl],
                                    lb_ref[:, sl], nw, tril, state_ref[hh])
        o_ref[:, sl] = out
        state_ref[hh] = new_state


def hgrn2(proj, lb, norm_w, offs, tc=512, heads=2):
    s = proj.shape[0]
    tc = min(tc, s)
    w = heads * HEAD_DIM
    assert all(o % heads == 0 for o in offs) and N_HEADS % heads == 0
    specs = [pl.BlockSpec((tc, w), functools.partial(lambda h, i, o: (i, o // heads + h), o=o)) for o in offs]
    return pl.pallas_call(
        _hgrn_kernel,
        grid=(N_HEADS // heads, s // tc),
        in_specs=specs + [pl.BlockSpec((1, w), lambda h, i: (0, h)),
                          pl.BlockSpec((1, HEAD_DIM), lambda h, i: (0, 0))],
        out_specs=pl.BlockSpec((tc, w), lambda h, i: (i, h)),
        out_shape=jax.ShapeDtypeStruct((s, HALF_WIDTH), BF16),
        scratch_shapes=[pltpu.VMEM((heads, HEAD_DIM, HEAD_DIM), F32)],
        compiler_params=_params("arbitrary", "arbitrary"),
        name="hgrn2",
    )(proj, proj, proj, proj, lb.reshape(1, HALF_WIDTH).astype(F32), norm_w.reshape(1, HEAD_DIM).astype(F32))


def rope_tables(s):
    half = ROPE_DIM // 2
    inv = ROPE_THETA ** (-jnp.arange(half, dtype=F32) / half)
    ang = jnp.arange(s).astype(F32)[:, None] * inv[None, :]
    cos, sin = jnp.cos(ang), jnp.sin(ang)
    one = jnp.ones((s, HEAD_DIM - ROPE_DIM), F32)
    zero = jnp.zeros((s, HEAD_DIM - ROPE_DIM), F32)
    z16 = jnp.zeros((s, half), F32)
    c = jnp.concatenate([cos, cos, one], 1)
    s_up = jnp.concatenate([z16, sin, zero], 1)
    s_dn = jnp.concatenate([-sin, z16, zero], 1)
    return c, s_up, s_dn


def _rope(x, c, s_up, s_dn):
    half = ROPE_DIM // 2
    return (x * c + pltpu.roll(x, half, axis=1) * s_up
            + pltpu.roll(x, HEAD_DIM - half, axis=1) * s_dn)


def _moba_prep_kernel(k_ref, v_ref, c_ref, su_ref, sd_ref, kr_ref, vt_ref, km_ref):
    c, su, sd = c_ref[...], su_ref[...], sd_ref[...]
    for h in range(N_HEADS):
        sl = slice(h * HEAD_DIM, (h + 1) * HEAD_DIM)
        kr = _rope(k_ref[:, sl], c, su, sd)
        kr_ref[:, sl] = kr.astype(BF16)
        km_ref[0, :, sl] = jnp.mean(kr, axis=0, keepdims=True)
    vt_ref[0] = v_ref[...].T.astype(BF16)


def moba_prep(proj, k_blk, v_blk, tables):
    s = proj.shape[0]
    nb = s // MOBA_BLOCK
    tab = pl.BlockSpec((MOBA_BLOCK, HEAD_DIM), lambda n: (n, 0))
    kr, vt, km = pl.pallas_call(
        _moba_prep_kernel,
        grid=(nb,),
        in_specs=[pl.BlockSpec((MOBA_BLOCK, HALF_WIDTH), lambda n: (n, k_blk)),
                  pl.BlockSpec((MOBA_BLOCK, HALF_WIDTH), lambda n: (n, v_blk)),
                  tab, tab, tab],
        out_specs=[pl.BlockSpec((MOBA_BLOCK, HALF_WIDTH), lambda n: (n, 0)),
                   pl.BlockSpec((1, HALF_WIDTH, MOBA_BLOCK), lambda n: (n, 0, 0)),
                   pl.BlockSpec((1, 1, HALF_WIDTH), lambda n: (n, 0, 0))],
        out_shape=[jax.ShapeDtypeStruct((s, HALF_WIDTH), BF16),
                   jax.ShapeDtypeStruct((nb, HALF_WIDTH, MOBA_BLOCK), BF16),
                   jax.ShapeDtypeStruct((nb, 1, HALF_WIDTH), F32)],
        compiler_params=_params("arbitrary"),
        name="moba_prep",
    )(proj, proj, *tables)
    return kr, vt, km.reshape(nb, HALF_WIDTH)


def _moba_kernel(q_ref, c_ref, su_ref, sd_ref, k_ref, vt_ref, km_ref, o_ref, sel_ref, s_ref, *, unroll):
    own = pl.program_id(1)
    blk, d = MOBA_BLOCK, HEAD_DIM
    nb = km_ref.shape[0]
    heads = q_ref.shape[1] // d
    cols = lambda hh: slice(hh * d, (hh + 1) * d)
    c, su, sd = c_ref[...], su_ref[...], sd_ref[...]
    o0 = pl.multiple_of(own * blk, blk)

    qs, init = [], []
    for hh in range(heads):
        qf = _rope(q_ref[:, cols(hh)], c, su, sd)
        q = (qf * d ** -0.5).astype(BF16)
        qs.append(q)
        gate = _nt(km_ref[:, cols(hh)], qf, precision=lax.Precision.HIGHEST)
        row = lax.broadcasted_iota(jnp.int32, gate.shape, 0)
        g = jnp.where(row < own, gate, NEG)
        sel = jnp.zeros(gate.shape, F32)
        for j in range(MOBA_TOPK):
            m = jnp.max(g, axis=0, keepdims=True)
            idx = jnp.min(jnp.where(g == m, row, nb), axis=0, keepdims=True)
            pick = row == idx
            sel = jnp.where(pick, jnp.where(j < own, 1.0, 0.0), sel)
            g = jnp.where(pick, -jnp.inf, g)
        sel_ref[hh] = sel
        s0 = _nt(k_ref[pl.ds(o0, blk), cols(hh)], q)
        kpos = lax.broadcasted_iota(jnp.int32, s0.shape, 0)
        qpos = lax.broadcasted_iota(jnp.int32, s0.shape, 1)
        s0 = jnp.where(kpos <= qpos, s0, NEG)
        m0 = jnp.max(s0, axis=0, keepdims=True)
        p0 = jnp.exp(s0 - m0)
        l0 = jnp.sum(p0, axis=0, keepdims=True)
        acc0 = jnp.dot(vt_ref[own, cols(hh), :], p0.astype(BF16), preferred_element_type=F32)
        init.append((m0, l0, acc0))

    def raw_scores(it):
        for hh in range(heads):
            for u in range(unroll):
                n = jnp.minimum(it * unroll + u, nb - 1)
                start = pl.multiple_of(n * blk, blk)
                s_ref[it % 2, hh, u] = _nt(k_ref[pl.ds(start, blk), cols(hh)], qs[hh])

    def body(it, carry):
        out = []
        for hh in range(heads):
            m, l, acc = carry[hh]
            scores = [jnp.where(sel_ref[hh, pl.ds(it * unroll + u, 1), :] > 0.0,
                                s_ref[it % 2, hh, u], NEG) for u in range(unroll)]
            m_new = functools.reduce(jnp.maximum, [jnp.max(sn, axis=0, keepdims=True) for sn in scores], m)
            alpha = jnp.exp(m - m_new)
            l, acc = alpha * l, alpha * acc
            for u, sn in enumerate(scores):
                p = jnp.exp(sn - m_new)
                l = l + jnp.sum(p, axis=0, keepdims=True)
                acc = acc + jnp.dot(vt_ref[it * unroll + u, cols(hh), :], p.astype(BF16),
                                    preferred_element_type=F32)
            out.append((m_new, l, acc))
        raw_scores(it + 1)
        return tuple(out)

    trips = (own + unroll - 1) // unroll
    raw_scores(0)
    final = lax.fori_loop(0, trips, body, tuple(init))
    for hh in range(heads):
        _, l, acc = final[hh]
        o_ref[:, cols(hh)] = (acc / l).T.astype(o_ref.dtype)


def moba(proj, q_off, kr, vt, kmean, tables, unroll=4, heads=2):
    s = proj.shape[0]
    nb = s // MOBA_BLOCK
    assert nb % unroll == 0 and q_off % heads == 0 and N_HEADS % heads == 0
    w = heads * HEAD_DIM
    tab = pl.BlockSpec((MOBA_BLOCK, HEAD_DIM), lambda h, i: (i, 0))
    return pl.pallas_call(
        functools.partial(_moba_kernel, unroll=unroll),
        grid=(N_HEADS // heads, nb),
        in_specs=[pl.BlockSpec((MOBA_BLOCK, w), lambda h, i: (i, q_off // heads + h)),
                  tab, tab, tab,
                  pl.BlockSpec((s, w), lambda h, i: (0, h)),
                  pl.BlockSpec((nb, w, MOBA_BLOCK), lambda h, i: (0, h, 0)),
                  pl.BlockSpec((nb, w), lambda h, i: (0, h))],
        out_specs=pl.BlockSpec((MOBA_BLOCK, w), lambda h, i: (i, h)),
        out_shape=jax.ShapeDtypeStruct((s, HALF_WIDTH), BF16),
        scratch_shapes=[pltpu.VMEM((heads, nb, MOBA_BLOCK), F32),
                        pltpu.VMEM((2, heads, unroll, MOBA_BLOCK, MOBA_BLOCK), F32)],
        compiler_params=_params("arbitrary", "arbitrary"),
        name="moba",
    )(proj, *tables, kr, vt, kmean)


def _even_mixer(h, w_in, w_out, e, s5_params, glu_w):
    qkv = matmul(h, w_in, (e,), BF16, col0=0, ncols=3 * HALF_WIDTH)
    u = matmul(h, w_in, (e,), F32, col0=3 * HALF_WIDTH, ncols=HALF_WIDTH)
    o_a = stick_breaking(qkv, 0, N_HEADS, 2 * N_HEADS)
    o_b = gelu_glu(s5_scan(u, s5_params), glu_w, (e,))
    return matmul_cat([o_a, o_b], w_out, (e,), BF16, tn=512)


def _odd_mixer(h, w_in, w_out, o, lb, norm_w):
    s = h.shape[0]
    proj = matmul(h, w_in, (o,), F32)
    o_c = hgrn2(proj, lb, norm_w, (0, N_HEADS, 2 * N_HEADS, 3 * N_HEADS))
    tables = rope_tables(s)
    kr, vt, kmean = moba_prep(proj, 5, 6, tables)
    o_d = moba(proj, 4 * N_HEADS, kr, vt, kmean, tables)
    return matmul_cat([o_c, o_d], w_out, (o,), BF16, tn=512)


def kernel(x, c, ada_w, ada_table, norm_pre, norm_post, ffn_w1, ffn_w3, ffn_w2, ev_w_in, ev_w_out,
           s5_a_re, s5_a_im, s5_b_re, s5_b_im, s5_c_re, s5_c_im, s5_d, s5_log_dt, s5_glu_w,
           od_w_in, od_w_out, hgrn_lb, hgrn_norm_w):
    bsz, seq, d = x.shape
    depth = ada_table.shape[0]
    mod_shared = ada_project(c, ada_w).reshape(bsz, 9, d)
    lb_cum = jnp.cumsum(jax.nn.softmax(hgrn_lb.astype(F32), axis=0), axis=0)
    lb_all = lb_cum - lb_cum[:1]

    outs = []
    for b in range(bsz):
        xb = x[b]
        mods = [mod_shared[b] + ada_table[layer] for layer in range(depth)]
        subs = [(layer, slot) for layer in range(depth) for slot in range(3)]
        first = subs[0]
        h = pre_norm(xb, norm_pre[first[0], first[1]], mods[first[0]][1], mods[first[0]][0])
        for n, (layer, slot) in enumerate(subs):
            mod = mods[layer]
            if slot == 1:
                if layer % 2 == 0:
                    e = layer // 2
                    y = _even_mixer(h, ev_w_in, ev_w_out, e,
                                    (s5_a_re[e], s5_a_im[e], s5_b_re[e], s5_b_im[e], s5_c_re[e], s5_c_im[e],
                                     s5_d[e], s5_log_dt[e]), s5_glu_w)
                else:
                    o = layer // 2
                    y = _odd_mixer(h, od_w_in, od_w_out, o, lb_all[layer], hgrn_norm_w[o])
                res_w = 1.0
            else:
                f = slot // 2
                g = gate_up(h, ffn_w1, ffn_w3, (layer, f))
                y = matmul_cat([g], ffn_w2, (layer, f), BF16)
                res_w = 0.5
            nxt = None
            if n + 1 < len(subs):
                nl, ns = subs[n + 1]
                nxt = (norm_pre[nl, ns], mods[nl][3 * ns + 1], mods[nl][3 * ns])
            xb, h = post_norm(y, xb, norm_post[layer, slot], mod[3 * slot + 2], res_w, nxt)
        outs.append(xb)
    return jnp.stack(outs, axis=0)
```

```python
import functools
import math

import jax
import jax.numpy as jnp
from jax import lax
from jax.experimental import pallas as pl
from jax.experimental.pallas import tpu as pltpu

F32 = jnp.float32
BF16 = jnp.bfloat16

HEAD_DIM = 128
N_HEADS = 16
HALF_WIDTH = N_HEADS * HEAD_DIM
SSM_GROUP = 16
SSM_GROUPS = HALF_WIDTH // SSM_GROUP
SSM_STATE = 64
SSM_CHUNK = 16
SSM_PACK = HEAD_DIM // SSM_GROUP
HGRN_SUB = 16
SB_BLOCK = 256
MOBA_BLOCK = 256
MOBA_TOPK = 3
ROPE_THETA = 500000.0
ROPE_DIM = HEAD_DIM // 4
RMS_EPS = 1e-6
NEG = -1e30
EXP_UNDERFLOW = -104.0

VMEM_LIMIT = 56 * 1024 * 1024


def _params(*sem):
    return pltpu.CompilerParams(dimension_semantics=sem, vmem_limit_bytes=VMEM_LIMIT)


def _nt(a, b, **kw):
    return lax.dot_general(a, b, (((1,), (1,)), ((), ())), preferred_element_type=F32, **kw)


def _tn(a, b):
    return lax.dot_general(a, b, (((0,), (0,)), ((), ())), preferred_element_type=F32)


def _split_bf16(x, parts=2):
    out = []
    for _ in range(parts):
        p = x.astype(BF16)
        out.append(p)
        x = x - p.astype(F32)
    return out


def _wspec(lead, block, index_map):
    lead = tuple(lead)
    return pl.BlockSpec((None,) * len(lead) + tuple(block), lambda *g: lead + tuple(index_map(*g)))


def _ada_kernel(c_ref, w_ref, o_ref):
    c = c_ref[...]
    a = c * jax.nn.sigmoid(c)
    o_ref[...] = jnp.dot(a, w_ref[...], precision=lax.Precision.HIGHEST,
                         preferred_element_type=F32)


def ada_project(c, ada_w, tn=1024):
    d, n = ada_w.shape
    bsz = c.shape[0]
    assert bsz <= 8
    c8 = jnp.zeros((8, d), F32).at[:bsz].set(c.astype(F32))
    out = pl.pallas_call(
        _ada_kernel,
        grid=(n // tn,),
        in_specs=[pl.BlockSpec((8, d), lambda j: (0, 0)),
                  pl.BlockSpec((d, tn), lambda j: (0, j))],
        out_specs=pl.BlockSpec((8, tn), lambda j: (0, j)),
        out_shape=jax.ShapeDtypeStruct((8, n), F32),
        compiler_params=_params("arbitrary"),
        name="ada_project",
    )(c8, ada_w)
    return out[:bsz]


def _rms(x, gain):
    return x * lax.rsqrt(jnp.mean(x * x, axis=-1, keepdims=True) + RMS_EPS) * gain


def _pre_kernel(x_ref, vec_ref, h_ref):
    h = _rms(x_ref[...], vec_ref[0:1, :]) * (1.0 + vec_ref[1:2, :]) + vec_ref[2:3, :]
    h_ref[...] = h.astype(h_ref.dtype)


def pre_norm(x, g_pre, scale, shift, tm=256):
    m, d = x.shape
    vec = jnp.zeros((8, d), F32).at[0].set(g_pre).at[1].set(scale).at[2].set(shift)
    return pl.pallas_call(
        _pre_kernel,
        grid=(m // tm,),
        in_specs=[pl.BlockSpec((tm, d), lambda i: (i, 0)),
                  pl.BlockSpec((8, d), lambda i: (0, 0))],
        out_specs=pl.BlockSpec((tm, d), lambda i: (i, 0)),
        out_shape=jax.ShapeDtypeStruct((m, d), BF16),
        compiler_params=_params("arbitrary"),
        name="pre_norm",
    )(x, vec)


def _post_kernel(y_ref, x_ref, vec_ref, xo_ref, *maybe_h_ref, res_w):
    yn = _rms(y_ref[...].astype(F32), vec_ref[0:1, :])
    xn = x_ref[...] + (res_w * vec_ref[1:2, :]) * yn
    xo_ref[...] = xn
    if maybe_h_ref:
        h = _rms(xn, vec_ref[2:3, :]) * (1.0 + vec_ref[3:4, :]) + vec_ref[4:5, :]
        maybe_h_ref[0][...] = h.astype(BF16)


def post_norm(y, x, g_post, gate, res_w, nxt=None, tm=256):
    m, d = x.shape
    vec = jnp.zeros((8, d), F32).at[0].set(g_post).at[1].set(gate)
    out_shape = [jax.ShapeDtypeStruct((m, d), F32)]
    out_specs = [pl.BlockSpec((tm, d), lambda i: (i, 0))]
    if nxt is not None:
        vec = vec.at[2].set(nxt[0]).at[3].set(nxt[1]).at[4].set(nxt[2])
        out_shape.append(jax.ShapeDtypeStruct((m, d), BF16))
        out_specs.append(pl.BlockSpec((tm, d), lambda i: (i, 0)))
    res = pl.pallas_call(
        functools.partial(_post_kernel, res_w=res_w),
        grid=(m // tm,),
        in_specs=[pl.BlockSpec((tm, d), lambda i: (i, 0)),
                  pl.BlockSpec((tm, d), lambda i: (i, 0)),
                  pl.BlockSpec((8, d), lambda i: (0, 0))],
        out_specs=out_specs,
        out_shape=out_shape,
        compiler_params=_params("arbitrary"),
        name="post_norm",
    )(y, x, vec)
    return (res[0], res[1]) if nxt is not None else (res[0], None)


def _mm_kernel(a_ref, w_ref, o_ref, *, tc):
    a = a_ref[...]
    for n0 in range(0, o_ref.shape[1], tc):
        o_ref[:, n0:n0 + tc] = jnp.dot(a, w_ref[:, n0:n0 + tc].astype(BF16),
                                       preferred_element_type=F32).astype(o_ref.dtype)


def matmul(a, w, lead, out_dtype, col0=0, ncols=None, tm=2048, tn=512):
    m, k = a.shape
    n = w.shape[-1] - col0 if ncols is None else ncols
    tm, tn = min(tm, m), min(tn, n)
    assert col0 % tn == 0 and n % tn == 0 and m % tm == 0
    j0 = col0 // tn
    return pl.pallas_call(
        functools.partial(_mm_kernel, tc=min(256, tn)),
        grid=(m // tm, n // tn),
        in_specs=[pl.BlockSpec((tm, k), lambda i, j: (i, 0), pipeline_mode=pl.Buffered(1)),
                  _wspec(lead, (k, tn), lambda i, j: (0, j + j0))],
        out_specs=pl.BlockSpec((tm, tn), lambda i, j: (i, j)),
        out_shape=jax.ShapeDtypeStruct((m, n), out_dtype),
        compiler_params=_params("arbitrary", "arbitrary"),
        name="matmul",
    )(a, w)


def _mm_cat_kernel(*refs, chunks):
    a_refs, w_ref, o_ref = refs[:len(chunks)], refs[len(chunks)], refs[len(chunks) + 1]
    acc = None
    row = 0
    for a_ref, kc in zip(a_refs, chunks):
        for k0 in range(0, a_ref.shape[1], kc):
            part = jnp.dot(a_ref[:, k0:k0 + kc], w_ref[row + k0:row + k0 + kc, :].astype(BF16),
                           preferred_element_type=F32)
            acc = part if acc is None else acc + part
        row += a_ref.shape[1]
    o_ref[...] = acc.astype(o_ref.dtype)


def _k_chunk(k, limit=6144):
    best = HEAD_DIM
    for c in range(HEAD_DIM, min(k, limit) + 1, HEAD_DIM):
        if k % c == 0:
            best = c
    return best


def matmul_cat(a_list, w, lead, out_dtype, tm=1024, tn=256):
    m = a_list[0].shape[0]
    k_total, n = w.shape[-2], w.shape[-1]
    assert sum(a.shape[1] for a in a_list) == k_total and m % tm == 0 and n % tn == 0
    chunks = tuple(_k_chunk(a.shape[1]) for a in a_list)
    a_specs = [pl.BlockSpec((tm, a.shape[1]), lambda i, j: (i, 0), pipeline_mode=pl.Buffered(1))
               for a in a_list]
    return pl.pallas_call(
        functools.partial(_mm_cat_kernel, chunks=chunks),
        grid=(m // tm, n // tn),
        in_specs=a_specs + [_wspec(lead, (k_total, tn), lambda i, j: (0, j))],
        out_specs=pl.BlockSpec((tm, tn), lambda i, j: (i, j)),
        out_shape=jax.ShapeDtypeStruct((m, n), out_dtype),
        compiler_params=_params("arbitrary", "arbitrary"),
        name="matmul_cat",
    )(*a_list, w)


def _gateup_kernel(a_ref, w1_ref, w3_ref, o_ref, *, tc):
    a = a_ref[...]
    for n0 in range(0, o_ref.shape[1], tc):
        g = jnp.dot(a, w1_ref[:, n0:n0 + tc].astype(BF16), preferred_element_type=F32)
        u = jnp.dot(a, w3_ref[:, n0:n0 + tc].astype(BF16), preferred_element_type=F32)
        o_ref[:, n0:n0 + tc] = (g * jax.nn.sigmoid(g) * u).astype(o_ref.dtype)


def gate_up(a, w1, w3, lead, tm=1024, tn=512):
    m, k = a.shape
    n = w1.shape[-1]
    tm, tn = min(tm, m), min(tn, n)
    wspec = _wspec(lead, (k, tn), lambda i, j: (0, j))
    return pl.pallas_call(
        functools.partial(_gateup_kernel, tc=min(256, tn)),
        grid=(m // tm, pl.cdiv(n, tn)),
        in_specs=[pl.BlockSpec((tm, k), lambda i, j: (i, 0), pipeline_mode=pl.Buffered(1)), wspec, wspec],
        out_specs=pl.BlockSpec((tm, tn), lambda i, j: (i, j)),
        out_shape=jax.ShapeDtypeStruct((m, n), BF16),
        compiler_params=_params("arbitrary", "arbitrary"),
        name="gate_up",
    )(a, w1, w3)


def _sb_kernel(q_ref, k_ref, v_ref, o_ref, *, t):
    i = pl.program_id(1)
    nch = q_ref.shape[0] // t
    qs = [q_ref[ch * t:(ch + 1) * t, :] for ch in range(nch)]
    row = lax.broadcasted_iota(jnp.int32, (t, t), 0)
    col = lax.broadcasted_iota(jnp.int32, (t, t), 1)
    suffix = (row > col).astype(BF16)
    scale = HEAD_DIM ** -0.5

    def cond(carry):
        step, cs, _ = carry
        top = functools.reduce(jnp.maximum, [jnp.max(c) for c in cs])
        return jnp.logical_and(nch * i + nch - 1 - step >= 0, top >= EXP_UNDERFLOW)

    def body(carry):
        step, cs, accs = carry
        new_c, new_acc = [], []
        for ch in range(nch):
            qblk = nch * i + ch
            j = qblk - step
            jc = jnp.maximum(j, 0)
            start = pl.multiple_of(jc * t, t)
            kj = k_ref[pl.ds(start, t), :]
            vj = v_ref[pl.ds(start, t), :]
            z = _nt(qs[ch], kj) * scale
            mask = (col - row) < jnp.where(j >= 0, (qblk - jc) * t, -2 * t)
            ls_pos = jnp.minimum(z, 0.0) - jnp.log(1.0 + jnp.exp(-jnp.abs(z)))
            lk = jnp.where(mask, ls_pos - z, 0.0)
            later = sum(jnp.dot(p, suffix, preferred_element_type=F32) for p in _split_bf16(lk))
            w = jnp.where(mask, jnp.exp(ls_pos + later + cs[ch]), 0.0)
            new_acc.append(accs[ch] + jnp.dot(w.astype(BF16), vj, preferred_element_type=F32))
            new_c.append(cs[ch] + jnp.sum(lk, axis=1, keepdims=True))
        return step + 1, tuple(new_c), tuple(new_acc)

    init = (jnp.int32(0), tuple(jnp.zeros((t, 1), F32) for _ in range(nch)),
            tuple(jnp.zeros((t, HEAD_DIM), F32) for _ in range(nch)))
    _, _, accs = lax.while_loop(cond, body, init)
    for ch in range(nch):
        o_ref[ch * t:(ch + 1) * t, :] = accs[ch].astype(o_ref.dtype)


def stick_breaking(proj, q_off, k_off, v_off, chains=2):
    s = proj.shape[0]
    t = min(SB_BLOCK, s)
    tq = min(chains * t, s)
    return pl.pallas_call(
        functools.partial(_sb_kernel, t=t),
        grid=(N_HEADS, s // tq),
        in_specs=[pl.BlockSpec((tq, HEAD_DIM), lambda h, i: (i, q_off + h)),
                  pl.BlockSpec((s, HEAD_DIM), lambda h, i: (0, k_off + h)),
                  pl.BlockSpec((s, HEAD_DIM), lambda h, i: (0, v_off + h))],
        out_specs=pl.BlockSpec((tq, HEAD_DIM), lambda h, i: (i, h)),
        out_shape=jax.ShapeDtypeStruct((s, HALF_WIDTH), BF16),
        compiler_params=_params("arbitrary", "arbitrary"),
        name="stick_breaking",
    )(proj, proj, proj)


def _s5_spread(rows, cols, row_period, col_period, row_div, col_div):
    r = lax.broadcasted_iota(jnp.int32, (rows, cols), 0)
    c = lax.broadcasted_iota(jnp.int32, (rows, cols), 1)
    same = jnp.logical_and(r // row_div == c // col_div, r % row_period == c % col_period)
    return same.astype(BF16)


def _s5_own_group(shape, rows_per_group, lanes_per_group):
    r = lax.broadcasted_iota(jnp.int32, shape, 0)
    c = lax.broadcasted_iota(jnp.int32, shape, 1)
    return (r // rows_per_group) % SSM_PACK == (c // lanes_per_group) % SSM_PACK


def _s5_expand(compact, spread, keep):
    full = jnp.dot(compact, spread, preferred_element_type=F32)
    return jnp.where(keep, full, 0.0).astype(BF16)


def _s5_kernel(u_ref, toep_ref, win_ref, vout_ref, ar_ref, ai_ref, y_ref, sin_ref, acc_ref):
    L, H, P = SSM_CHUNK, SSM_GROUP, SSM_STATE
    nc = u_ref.shape[0] // L
    jc = pl.program_id(1)
    per = acc_ref.shape[1] // HEAD_DIM
    xp = [jnp.concatenate([u_ref[pl.ds(2 * p, nc, stride=L), :], u_ref[pl.ds(2 * p + 1, nc, stride=L), :]],
                          axis=1).astype(BF16) for p in range(L // 2)]
    pair = lambda ref, p: ref[0, 0, 2 * p:2 * p + 2].reshape(2 * HEAD_DIM, ref.shape[-1])

    @pl.when(jc == 0)
    def _():
        nstate = sin_ref.shape[1]
        half = nstate // 2
        spread = _s5_spread(2 * P, nstate, P, P, P, half)
        keep = _s5_own_group((2 * HEAD_DIM, nstate), H, P)
        b = sum(jnp.dot(xp[p], _s5_expand(pair(win_ref, p), spread, keep), preferred_element_type=F32)
                for p in range(L // 2))
        er, ei = b[:, :half], b[:, half:]
        row = lax.broadcasted_iota(jnp.int32, er.shape, 0)
        for k in range(int(math.log2(nc))):
            sh = 1 << k
            sr = jnp.where(row >= sh, pltpu.roll(er, sh, axis=0), 0.0)
            si = jnp.where(row >= sh, pltpu.roll(ei, sh, axis=0), 0.0)
            ar, ai = ar_ref[0, k:k + 1, :], ai_ref[0, k:k + 1, :]
            er, ei = er + ar * sr - ai * si, ei + ar * si + ai * sr
        sin_ref[:, :half] = jnp.where(row >= 1, pltpu.roll(er, 1, axis=0), 0.0).astype(BF16)
        sin_ref[:, half:] = jnp.where(row >= 1, pltpu.roll(ei, 1, axis=0), 0.0).astype(BF16)

    spread = _s5_spread(per * H, per * HEAD_DIM, H, H, H, HEAD_DIM)
    keep_state = _s5_own_group((sin_ref.shape[1], per * HEAD_DIM), P, H)
    acc_ref[...] = jnp.dot(sin_ref[...], _s5_expand(vout_ref[0, 0], spread, keep_state),
                           preferred_element_type=F32)
    keep_in = _s5_own_group((2 * HEAD_DIM, per * HEAD_DIM), H, H)
    for p in range(L // 2):
        @pl.when(2 * p < (jc + 1) * per)
        def _(p=p):
            acc_ref[...] += jnp.dot(xp[p], _s5_expand(pair(toep_ref, p), spread, keep_in),
                                    preferred_element_type=F32)
    for t in range(per):
        y_ref[pl.ds(jc * per + t, nc, stride=L), :] = acc_ref[:, t * HEAD_DIM:(t + 1) * HEAD_DIM]


def s5_tables(a_re, a_im, b_re, b_im, c_re, c_im, d_skip, log_dt, ncol):
    L, P, H, G, K8 = SSM_CHUNK, SSM_STATE, SSM_GROUP, SSM_GROUPS, SSM_PACK
    GC = G // K8
    per = L // ncol
    dt = jnp.exp(log_dt.astype(F32))[:, None]
    ar, ai = a_re.astype(F32), a_im.astype(F32)
    mag = jnp.exp(ar * dt)
    lr, li = mag * jnp.cos(ai * dt), mag * jnp.sin(ai * dt)
    den = ar * ar + ai * ai
    nr, ni = lr - 1.0, li
    cr = (nr * ar + ni * ai) / den
    ci = (ni * ar - nr * ai) / den
    br, bi = b_re.astype(F32), b_im.astype(F32)
    bbr = cr[..., None] * br - ci[..., None] * bi
    bbi = cr[..., None] * bi + ci[..., None] * br
    pr, pi = [jnp.ones_like(lr)], [jnp.zeros_like(li)]
    for _ in range(L):
        pr, pi = pr + [pr[-1] * lr - pi[-1] * li], pi + [pr[-1] * li + pi[-1] * lr]
    pr, pi = jnp.stack(pr, 1), jnp.stack(pi, 1)
    ccr, cci = c_re.astype(F32), c_im.astype(F32)
    clr = ccr[:, None] * pr[:, :L, None, :] - cci[:, None] * pi[:, :L, None, :]
    cli = ccr[:, None] * pi[:, :L, None, :] + cci[:, None] * pr[:, :L, None, :]
    kern = (jnp.einsum('gdnp,gph->gdnh', clr, bbr, precision='highest')
            - jnp.einsum('gdnp,gph->gdnh', cli, bbi, precision='highest'))
    kern = kern.at[:, 0].add(jnp.eye(H, dtype=F32)[None] * d_skip.astype(F32).reshape(G, H)[:, :, None])
    tau = jnp.arange(L)
    diff = tau[None, :] - tau[:, None]
    kt = kern[:, jnp.clip(diff, 0, L - 1)]
    kt = jnp.where((diff >= 0)[None, :, :, None, None], kt, 0.0)
    toep = kt.reshape(GC, K8, L, ncol, per, H, H).transpose(0, 3, 2, 1, 6, 4, 5)
    toep = toep.reshape(GC, ncol, L, K8 * H, per * H)
    qr, qi = pr[:, L - 1 - tau], pi[:, L - 1 - tau]
    bt_r, bt_i = bbr.transpose(0, 2, 1)[:, None], bbi.transpose(0, 2, 1)[:, None]
    wre = qr[:, :, None, :] * bt_r - qi[:, :, None, :] * bt_i
    wim = qr[:, :, None, :] * bt_i + qi[:, :, None, :] * bt_r
    win = jnp.concatenate([wre, wim], -1)
    win = win.reshape(GC, K8, L, H, 2 * P).transpose(0, 2, 1, 3, 4).reshape(GC, 1, L, K8 * H, 2 * P)
    pr1, pi1 = pr[:, 1:], pi[:, 1:]
    vre = ccr[:, None] * pr1[:, :, None, :] - cci[:, None] * pi1[:, :, None, :]
    vim = ccr[:, None] * pi1[:, :, None, :] + cci[:, None] * pr1[:, :, None, :]
    vout = jnp.stack([vre, -vim], 1)
    vout = vout.reshape(GC, K8, 2, ncol, per, H, P).transpose(0, 3, 2, 1, 6, 4, 5)
    vout = vout.reshape(GC, ncol, 2 * K8 * P, per * H)
    zr, zi = [pr[:, L]], [pi[:, L]]
    for _ in range(15):
        zr, zi = zr + [zr[-1] * zr[-1] - zi[-1] * zi[-1]], zi + [2.0 * zr[-1] * zi[-1]]
    pack = lambda z: jnp.stack(z, 1).reshape(GC, K8, 16, P).transpose(0, 2, 1, 3).reshape(GC, 16, K8 * P)
    return toep.astype(BF16), win.astype(BF16), vout.astype(BF16), pack(zr), pack(zi)


def s5_scan(u, s5_params, ncol=2):
    toep, win, vout, zr, zi = s5_tables(*s5_params, ncol=ncol)
    s, width = u.shape
    L, K8, P = SSM_CHUNK, SSM_PACK, SSM_STATE
    gc = width // HEAD_DIM
    cw = (L // ncol) * HEAD_DIM
    nstate = 2 * K8 * P
    return pl.pallas_call(
        _s5_kernel,
        grid=(gc, ncol),
        in_specs=[pl.BlockSpec((s, HEAD_DIM), lambda g, j: (0, g)),
                  pl.BlockSpec((1, 1) + toep.shape[2:], lambda g, j: (g, j, 0, 0, 0)),
                  pl.BlockSpec((1, 1) + win.shape[2:], lambda g, j: (g, 0, 0, 0, 0)),
                  pl.BlockSpec((1, 1) + vout.shape[2:], lambda g, j: (g, j, 0, 0)),
                  pl.BlockSpec((1, 16, nstate // 2), lambda g, j: (g, 0, 0)),
                  pl.BlockSpec((1, 16, nstate // 2), lambda g, j: (g, 0, 0))],
        out_specs=pl.BlockSpec((s, HEAD_DIM), lambda g, j: (0, g)),
        out_shape=jax.ShapeDtypeStruct((s, width), F32),
        scratch_shapes=[pltpu.VMEM((s // L, nstate), BF16), pltpu.VMEM((s // L, cw), F32)],
        compiler_params=_params("arbitrary", "arbitrary"),
        name="s5_scan",
    )(u, toep, win, vout, zr, zi)


def _glu_kernel(y_ref, yj_ref, w_ref, o_ref):
    yg = jax.nn.gelu(y_ref[...])
    z = jnp.dot(yg.astype(BF16), w_ref[...].astype(BF16), preferred_element_type=F32)
    o_ref[...] = (jax.nn.gelu(yj_ref[...]) * jax.nn.sigmoid(z)).astype(o_ref.dtype)


def gelu_glu(y, glu_w, lead, tm=512, tn=512):
    m, k = y.shape
    tm = min(tm, m)
    return pl.pallas_call(
        _glu_kernel,
        grid=(m // tm, k // tn),
        in_specs=[pl.BlockSpec((tm, k), lambda i, j: (i, 0)),
                  pl.BlockSpec((tm, tn), lambda i, j: (i, j)),
                  _wspec(lead, (k, tn), lambda i, j: (0, j))],
        out_specs=pl.BlockSpec((tm, tn), lambda i, j: (i, j)),
        out_shape=jax.ShapeDtypeStruct((m, k), BF16),
        compiler_params=_params("arbitrary", "arbitrary"),
        name="gelu_glu",
    )(y, y, glu_w)


def _hgrn_head(q, f_logit, v, gate, lb, nw, tril, state):
    tc, d = q.shape
    sub = HGRN_SUB
    nblk = tc // sub
    f = lb + (1.0 - lb) * jax.nn.sigmoid(f_logit)
    k = 1.0 - f
    gcum = sum(jnp.dot(tril, p, preferred_element_type=F32)
               for p in _split_bf16(jnp.log(f), parts=3))
    gtot = gcum[tc - 1:tc, :]
    o = _nt((q * jnp.exp(gcum)).astype(BF16), state.astype(BF16))

    g3, q3, k3, v3 = (x.reshape(nblk, sub, d) for x in (gcum, q, k, v))
    s_idx = lax.broadcasted_iota(jnp.int32, (nblk, sub, d), 1)
    rows = []
    for t in range(sub):
        dec = jnp.exp(jnp.where(s_idx <= t, g3[:, t:t + 1] - g3, -jnp.inf))
        a = jnp.sum(q3[:, t:t + 1] * k3 * dec, axis=2, keepdims=True)
        rows.append(jnp.sum(a * v3, axis=1, keepdims=True))
    o = o + jnp.concatenate(rows, axis=1).reshape(tc, d)

    v16 = v.astype(BF16)
    b = sub
    while 2 * b <= tc:
        m = tc // (2 * b)
        g4, q4, k4 = (x.reshape(m, 2 * b, d) for x in (gcum, q, k))
        ref = g4[:, b:b + 1]
        qd = (q4[:, b:] * jnp.exp(g4[:, b:] - ref)).astype(BF16)
        kd = (k4[:, :b] * jnp.exp(ref - g4[:, :b])).astype(BF16)
        att = jnp.einsum('mqd,mkd->mqk', qd, kd, preferred_element_type=F32)
        ob = jnp.einsum('mqk,mkd->mqd', att.astype(BF16), v16.reshape(m, 2 * b, d)[:, :b],
                        preferred_element_type=F32)
        o = o + jnp.concatenate([jnp.zeros_like(ob), ob], axis=1).reshape(tc, d)
        b *= 2

    kdec = (k * jnp.exp(gtot - gcum)).astype(BF16)
    new_state = state * jnp.exp(gtot) + _tn(v16, kdec)
    on = o * lax.rsqrt(jnp.mean(o * o, axis=-1, keepdims=True) + RMS_EPS) * nw
    return (on * (gate * jax.nn.sigmoid(gate))).astype(BF16), new_state


def _hgrn_kernel(q_ref, f_ref, i_ref, g_ref, lb_ref, nw_ref, o_ref, state_ref):
    @pl.when(pl.program_id(1) == 0)
    def _():
        state_ref[...] = jnp.zeros_like(state_ref)

    tc = q_ref.shape[0]
    d = HEAD_DIM
    r_i = lax.broadcasted_iota(jnp.int32, (tc, tc), 0)
    c_i = lax.broadcasted_iota(jnp.int32, (tc, tc), 1)
    tril = (c_i <= r_i).astype(BF16)
    nw = nw_ref[...]
    for hh in range(q_ref.shape[1] // d):
        sl = slice(hh * d, (hh + 1) * d)
        out, new_state = _hgrn_head(q_ref[:, sl], f_ref[:, sl], i_ref[:, sl], g_ref[:, sl],
                                    lb_ref[:, sl], nw, tril, state_ref[hh])
        o_ref[:, sl] = out
        state_ref[hh] = new_state


def hgrn2(proj, lb, norm_w, offs, tc=512, heads=2):
    s = proj.shape[0]
    tc = min(tc, s)
    w = heads * HEAD_DIM
    assert all(o % heads == 0 for o in offs) and N_HEADS % heads == 0
    specs = [pl.BlockSpec((tc, w), functools.partial(lambda h, i, o: (i, o // heads + h), o=o)) for o in offs]
    return pl.pallas_call(
        _hgrn_kernel,
        grid=(N_HEADS // heads, s // tc),
        in_specs=specs + [pl.BlockSpec((1, w), lambda h, i: (0, h)),
                          pl.BlockSpec((1, HEAD_DIM), lambda h, i: (0, 0))],
        out_specs=pl.BlockSpec((tc, w), lambda h, i: (i, h)),
        out_shape=jax.ShapeDtypeStruct((s, HALF_WIDTH), BF16),
        scratch_shapes=[pltpu.VMEM((heads, HEAD_DIM, HEAD_DIM), F32)],
        compiler_params=_params("arbitrary", "arbitrary"),
        name="hgrn2",
    )(proj, proj, proj, proj, lb.reshape(1, HALF_WIDTH).astype(F32), norm_w.reshape(1, HEAD_DIM).astype(F32))


def rope_tables(s):
    half = ROPE_DIM // 2
    inv = ROPE_THETA ** (-jnp.arange(half, dtype=F32) / half)
    ang = jnp.arange(s).astype(F32)[:, None] * inv[None, :]
    cos, sin = jnp.cos(ang), jnp.sin(ang)
    one = jnp.ones((s, HEAD_DIM - ROPE_DIM), F32)
    zero = jnp.zeros((s, HEAD_DIM - ROPE_DIM), F32)
    z16 = jnp.zeros((s, half), F32)
    c = jnp.concatenate([cos, cos, one], 1)
    s_up = jnp.concatenate([z16, sin, zero], 1)
    s_dn = jnp.concatenate([-sin, z16, zero], 1)
    return c, s_up, s_dn


def _rope(x, c, s_up, s_dn):
    half = ROPE_DIM // 2
    return (x * c + pltpu.roll(x, half, axis=1) * s_up
            + pltpu.roll(x, HEAD_DIM - half, axis=1) * s_dn)


def _moba_prep_kernel(k_ref, v_ref, c_ref, su_ref, sd_ref, kr_ref, vt_ref, km_ref):
    c, su, sd = c_ref[...], su_ref[...], sd_ref[...]
    for h in range(N_HEADS):
        sl = slice(h * HEAD_DIM, (h + 1) * HEAD_DIM)
        kr = _rope(k_ref[:, sl], c, su, sd)
        kr_ref[:, sl] = kr.astype(BF16)
        km_ref[0, :, sl] = jnp.mean(kr, axis=0, keepdims=True)
    vt_ref[0] = v_ref[...].T.astype(BF16)


def moba_prep(proj, k_blk, v_blk, tables):
    s = proj.shape[0]
    nb = s // MOBA_BLOCK
    tab = pl.BlockSpec((MOBA_BLOCK, HEAD_DIM), lambda n: (n, 0))
    kr, vt, km = pl.pallas_call(
        _moba_prep_kernel,
        grid=(nb,),
        in_specs=[pl.BlockSpec((MOBA_BLOCK, HALF_WIDTH), lambda n: (n, k_blk)),
                  pl.BlockSpec((MOBA_BLOCK, HALF_WIDTH), lambda n: (n, v_blk)),
                  tab, tab, tab],
        out_specs=[pl.BlockSpec((MOBA_BLOCK, HALF_WIDTH), lambda n: (n, 0)),
                   pl.BlockSpec((1, HALF_WIDTH, MOBA_BLOCK), lambda n: (n, 0, 0)),
                   pl.BlockSpec((1, 1, HALF_WIDTH), lambda n: (n, 0, 0))],
        out_shape=[jax.ShapeDtypeStruct((s, HALF_WIDTH), BF16),
                   jax.ShapeDtypeStruct((nb, HALF_WIDTH, MOBA_BLOCK), BF16),
                   jax.ShapeDtypeStruct((nb, 1, HALF_WIDTH), F32)],
        compiler_params=_params("arbitrary"),
        name="moba_prep",
    )(proj, proj, *tables)
    return kr, vt, km.reshape(nb, HALF_WIDTH)


def _moba_kernel(q_ref, c_ref, su_ref, sd_ref, k_ref, vt_ref, km_ref, o_ref, sel_ref, s_ref, *, unroll):
    own = pl.program_id(1)
    blk, d = MOBA_BLOCK, HEAD_DIM
    nb = km_ref.shape[0]
    heads = q_ref.shape[1] // d
    cols = lambda hh: slice(hh * d, (hh + 1) * d)
    c, su, sd = c_ref[...], su_ref[...], sd_ref[...]
    o0 = pl.multiple_of(own * blk, blk)

    qs, init = [], []
    for hh in range(heads):
        qf = _rope(q_ref[:, cols(hh)], c, su, sd)
        q = (qf * d ** -0.5).astype(BF16)
        qs.append(q)
        gate = _nt(km_ref[:, cols(hh)], qf, precision=lax.Precision.HIGHEST)
        row = lax.broadcasted_iota(jnp.int32, gate.shape, 0)
        g = jnp.where(row < own, gate, NEG)
        sel = jnp.zeros(gate.shape, F32)
        for j in range(MOBA_TOPK):
            m = jnp.max(g, axis=0, keepdims=True)
            idx = jnp.min(jnp.where(g == m, row, nb), axis=0, keepdims=True)
            pick = row == idx
            sel = jnp.where(pick, jnp.where(j < own, 1.0, 0.0), sel)
            g = jnp.where(pick, -jnp.inf, g)
        sel_ref[hh] = sel
        s0 = _nt(k_ref[pl.ds(o0, blk), cols(hh)], q)
        kpos = lax.broadcasted_iota(jnp.int32, s0.shape, 0)
        qpos = lax.broadcasted_iota(jnp.int32, s0.shape, 1)
        s0 = jnp.where(kpos <= qpos, s0, NEG)
        m0 = jnp.max(s0, axis=0, keepdims=True)
        p0 = jnp.exp(s0 - m0)
        l0 = jnp.sum(p0, axis=0, keepdims=True)
        acc0 = jnp.dot(vt_ref[own, cols(hh), :], p0.astype(BF16), preferred_element_type=F32)
        init.append((m0, l0, acc0))

    def raw_scores(it):
        for hh in range(heads):
            for u in range(unroll):
                n = jnp.minimum(it * unroll + u, nb - 1)
                start = pl.multiple_of(n * blk, blk)
                s_ref[it % 2, hh, u] = _nt(k_ref[pl.ds(start, blk), cols(hh)], qs[hh])

    def body(it, carry):
        out = []
        for hh in range(heads):
            m, l, acc = carry[hh]
            scores = [jnp.where(sel_ref[hh, pl.ds(it * unroll + u, 1), :] > 0.0,
                                s_ref[it % 2, hh, u], NEG) for u in range(unroll)]
            m_new = functools.reduce(jnp.maximum, [jnp.max(sn, axis=0, keepdims=True) for sn in scores], m)
            alpha = jnp.exp(m - m_new)
            l, acc = alpha * l, alpha * acc
            for u, sn in enumerate(scores):
                p = jnp.exp(sn - m_new)
                l = l + jnp.sum(p, axis=0, keepdims=True)
                acc = acc + jnp.dot(vt_ref[it * unroll + u, cols(hh), :], p.astype(BF16),
                                    preferred_element_type=F32)
            out.append((m_new, l, acc))
        raw_scores(it + 1)
        return tuple(out)

    trips = (own + unroll - 1) // unroll
    raw_scores(0)
    final = lax.fori_loop(0, trips, body, tuple(init))
    for hh in range(heads):
        _, l, acc = final[hh]
        o_ref[:, cols(hh)] = (acc / l).T.astype(o_ref.dtype)


def moba(proj, q_off, kr, vt, kmean, tables, unroll=4, heads=2):
    s = proj.shape[0]
    nb = s // MOBA_BLOCK
    assert nb % unroll == 0 and q_off % heads == 0 and N_HEADS % heads == 0
    w = heads * HEAD_DIM
    tab = pl.BlockSpec((MOBA_BLOCK, HEAD_DIM), lambda h, i: (i, 0))
    return pl.pallas_call(
        functools.partial(_moba_kernel, unroll=unroll),
        grid=(N_HEADS // heads, nb),
        in_specs=[pl.BlockSpec((MOBA_BLOCK, w), lambda h, i: (i, q_off // heads + h)),
                  tab, tab, tab,
                  pl.BlockSpec((s, w), lambda h, i: (0, h)),
                  pl.BlockSpec((nb, w, MOBA_BLOCK), lambda h, i: (0, h, 0)),
                  pl.BlockSpec((nb, w), lambda h, i: (0, h))],
        out_specs=pl.BlockSpec((MOBA_BLOCK, w), lambda h, i: (i, h)),
        out_shape=jax.ShapeDtypeStruct((s, HALF_WIDTH), BF16),
        scratch_shapes=[pltpu.VMEM((heads, nb, MOBA_BLOCK), F32),
                        pltpu.VMEM((2, heads, unroll, MOBA_BLOCK, MOBA_BLOCK), F32)],
        compiler_params=_params("arbitrary", "arbitrary"),
        name="moba",
    )(proj, *tables, kr, vt, kmean)


def _even_mixer(h, w_in, w_out, e, s5_params, glu_w):
    qkv = matmul(h, w_in, (e,), BF16, col0=0, ncols=3 * HALF_WIDTH)
    u = matmul(h, w_in, (e,), F32, col0=3 * HALF_WIDTH, ncols=HALF_WIDTH)
    o_a = stick_breaking(qkv, 0, N_HEADS, 2 * N_HEADS)
    o_b = gelu_glu(s5_scan(u, s5_params), glu_w, (e,))
    return matmul_cat([o_a, o_b], w_out, (e,), BF16, tn=512)


def _odd_mixer(h, w_in, w_out, o, lb, norm_w):
    s = h.shape[0]
    proj = matmul(h, w_in, (o,), F32)
    o_c = hgrn2(proj, lb, norm_w, (0, N_HEADS, 2 * N_HEADS, 3 * N_HEADS))
    tables = rope_tables(s)
    kr, vt, kmean = moba_prep(proj, 5, 6, tables)
    o_d = moba(proj, 4 * N_HEADS, kr, vt, kmean, tables)
    return matmul_cat([o_c, o_d], w_out, (o,), BF16, tn=512)


def kernel(x, c, ada_w, ada_table, norm_pre, norm_post, ffn_w1, ffn_w3, ffn_w2, ev_w_in, ev_w_out,
           s5_a_re, s5_a_im, s5_b_re, s5_b_im, s5_c_re, s5_c_im, s5_d, s5_log_dt, s5_glu_w,
           od_w_in, od_w_out, hgrn_lb, hgrn_norm_w):
    bsz, seq, d = x.shape
    depth = ada_table.shape[0]
    mod_shared = ada_project(c, ada_w).reshape(bsz, 9, d)
    lb_cum = jnp.cumsum(jax.nn.softmax(hgrn_lb.astype(F32), axis=0), axis=0)
    lb_all = lb_cum - lb_cum[:1]

    outs = []
    for b in range(bsz):
        xb = x[b]
        mods = [mod_shared[b] + ada_table[layer] for layer in range(depth)]
        subs = [(layer, slot) for layer in range(depth) for slot in range(3)]
        first = subs[0]
        h = pre_norm(xb, norm_pre[first[0], first[1]], mods[first[0]][1], mods[first[0]][0])
        for n, (layer, slot) in enumerate(subs):
            mod = mods[layer]
            if slot == 1:
                if layer % 2 == 0:
                    e = layer // 2
                    y = _even_mixer(h, ev_w_in, ev_w_out, e,
                                    (s5_a_re[e], s5_a_im[e], s5_b_re[e], s5_b_im[e], s5_c_re[e], s5_c_im[e],
                                     s5_d[e], s5_log_dt[e]), s5_glu_w)
                else:
                    o = layer // 2
                    y = _odd_mixer(h, od_w_in, od_w_out, o, lb_all[layer], hgrn_norm_w[o])
                res_w = 1.0
            else:
                f = slot // 2
                g = gate_up(h, ffn_w1, ffn_w3, (layer, f))
                y = matmul_cat([g], ffn_w2, (layer, f), BF16)
                res_w = 0.5
            nxt = None
            if n + 1 < len(subs):
                nl, ns = subs[n + 1]
                nxt = (norm_pre[nl, ns], mods[nl][3 * ns + 1], mods[nl][3 * ns])
            xb, h = post_norm(y, xb, norm_post[layer, slot], mod[3 * slot + 2], res_w, nxt)
        outs.append(xb)
    return jnp.stack(outs, axis=0)
```

```python
import functools
import math

import jax
import jax.numpy as jnp
from jax import lax
from jax.experimental import pallas as pl
from jax.experimental.pallas import tpu as pltpu

F32 = jnp.float32
BF16 = jnp.bfloat16

HEAD_DIM = 128
N_HEADS = 16
HALF_WIDTH = N_HEADS * HEAD_DIM
SSM_GROUP = 16
SSM_GROUPS = HALF_WIDTH // SSM_GROUP
SSM_STATE = 64
SSM_CHUNK = 16
SSM_PACK = HEAD_DIM // SSM_GROUP
HGRN_SUB = 16
SB_BLOCK = 256
MOBA_BLOCK = 256
MOBA_TOPK = 3
ROPE_THETA = 500000.0
ROPE_DIM = HEAD_DIM // 4
RMS_EPS = 1e-6
NEG = -1e30
EXP_UNDERFLOW = -104.0

VMEM_LIMIT = 56 * 1024 * 1024


def _params(*sem):
    return pltpu.CompilerParams(dimension_semantics=sem, vmem_limit_bytes=VMEM_LIMIT)


def _nt(a, b, **kw):
    return lax.dot_general(a, b, (((1,), (1,)), ((), ())), preferred_element_type=F32, **kw)


def _tn(a, b):
    return lax.dot_general(a, b, (((0,), (0,)), ((), ())), preferred_element_type=F32)


def _split_bf16(x, parts=2):
    out = []
    for _ in range(parts):
        p = x.astype(BF16)
        out.append(p)
        x = x - p.astype(F32)
    return out


def _wspec(lead, block, index_map):
    lead = tuple(lead)
    return pl.BlockSpec((None,) * len(lead) + tuple(block), lambda *g: lead + tuple(index_map(*g)))


def _ada_kernel(c_ref, w_ref, o_ref):
    c = c_ref[...]
    a = c * jax.nn.sigmoid(c)
    o_ref[...] = jnp.dot(a, w_ref[...], precision=lax.Precision.HIGHEST,
                         preferred_element_type=F32)


def ada_project(c, ada_w, tn=1024):
    d, n = ada_w.shape
    bsz = c.shape[0]
    assert bsz <= 8
    c8 = jnp.zeros((8, d), F32).at[:bsz].set(c.astype(F32))
    out = pl.pallas_call(
        _ada_kernel,
        grid=(n // tn,),
        in_specs=[pl.BlockSpec((8, d), lambda j: (0, 0)),
                  pl.BlockSpec((d, tn), lambda j: (0, j))],
        out_specs=pl.BlockSpec((8, tn), lambda j: (0, j)),
        out_shape=jax.ShapeDtypeStruct((8, n), F32),
        compiler_params=_params("arbitrary"),
        name="ada_project",
    )(c8, ada_w)
    return out[:bsz]


def _rms(x, gain):
    return x * lax.rsqrt(jnp.mean(x * x, axis=-1, keepdims=True) + RMS_EPS) * gain


def _pre_kernel(x_ref, vec_ref, h_ref):
    h = _rms(x_ref[...], vec_ref[0:1, :]) * (1.0 + vec_ref[1:2, :]) + vec_ref[2:3, :]
    h_ref[...] = h.astype(h_ref.dtype)


def pre_norm(x, g_pre, scale, shift, tm=256):
    m, d = x.shape
    vec = jnp.zeros((8, d), F32).at[0].set(g_pre).at[1].set(scale).at[2].set(shift)
    return pl.pallas_call(
        _pre_kernel,
        grid=(m // tm,),
        in_specs=[pl.BlockSpec((tm, d), lambda i: (i, 0)),
                  pl.BlockSpec((8, d), lambda i: (0, 0))],
        out_specs=pl.BlockSpec((tm, d), lambda i: (i, 0)),
        out_shape=jax.ShapeDtypeStruct((m, d), BF16),
        compiler_params=_params("arbitrary"),
        name="pre_norm",
    )(x, vec)


def _post_kernel(y_ref, x_ref, vec_ref, xo_ref, *maybe_h_ref, res_w):
    yn = _rms(y_ref[...].astype(F32), vec_ref[0:1, :])
    xn = x_ref[...] + (res_w * vec_ref[1:2, :]) * yn
    xo_ref[...] = xn
    if maybe_h_ref:
        h = _rms(xn, vec_ref[2:3, :]) * (1.0 + vec_ref[3:4, :]) + vec_ref[4:5, :]
        maybe_h_ref[0][...] = h.astype(BF16)


def post_norm(y, x, g_post, gate, res_w, nxt=None, tm=256):
    m, d = x.shape
    vec = jnp.zeros((8, d), F32).at[0].set(g_post).at[1].set(gate)
    out_shape = [jax.ShapeDtypeStruct((m, d), F32)]
    out_specs = [pl.BlockSpec((tm, d), lambda i: (i, 0))]
    if nxt is not None:
        vec = vec.at[2].set(nxt[0]).at[3].set(nxt[1]).at[4].set(nxt[2])
        out_shape.append(jax.ShapeDtypeStruct((m, d), BF16))
        out_specs.append(pl.BlockSpec((tm, d), lambda i: (i, 0)))
    res = pl.pallas_call(
        functools.partial(_post_kernel, res_w=res_w),
        grid=(m // tm,),
        in_specs=[pl.BlockSpec((tm, d), lambda i: (i, 0)),
                  pl.BlockSpec((tm, d), lambda i: (i, 0)),
                  pl.BlockSpec((8, d), lambda i: (0, 0))],
        out_specs=out_specs,
        out_shape=out_shape,
        compiler_params=_params("arbitrary"),
        name="post_norm",
    )(y, x, vec)
    return (res[0], res[1]) if nxt is not None else (res[0], None)


def _mm_kernel(a_ref, w_ref, o_ref, *, tc):
    a = a_ref[...]
    for n0 in range(0, o_ref.shape[1], tc):
        o_ref[:, n0:n0 + tc] = jnp.dot(a, w_ref[:, n0:n0 + tc].astype(BF16),
                                       preferred_element_type=F32).astype(o_ref.dtype)


def matmul(a, w, lead, out_dtype, col0=0, ncols=None, tm=2048, tn=512):
    m, k = a.shape
    n = w.shape[-1] - col0 if ncols is None else ncols
    tm, tn = min(tm, m), min(tn, n)
    assert col0 % tn == 0 and n % tn == 0 and m % tm == 0
    j0 = col0 // tn
    return pl.pallas_call(
        functools.partial(_mm_kernel, tc=min(256, tn)),
        grid=(m // tm, n // tn),
        in_specs=[pl.BlockSpec((tm, k), lambda i, j: (i, 0), pipeline_mode=pl.Buffered(1)),
                  _wspec(lead, (k, tn), lambda i, j: (0, j + j0))],
        out_specs=pl.BlockSpec((tm, tn), lambda i, j: (i, j)),
        out_shape=jax.ShapeDtypeStruct((m, n), out_dtype),
        compiler_params=_params("arbitrary", "arbitrary"),
        name="matmul",
    )(a, w)


def _mm_cat_kernel(*refs, chunks):
    a_refs, w_ref, o_ref = refs[:len(chunks)], refs[len(chunks)], refs[len(chunks) + 1]
    acc = None
    row = 0
    for a_ref, kc in zip(a_refs, chunks):
        for k0 in range(0, a_ref.shape[1], kc):
            part = jnp.dot(a_ref[:, k0:k0 + kc], w_ref[row + k0:row + k0 + kc, :].astype(BF16),
                           preferred_element_type=F32)
            acc = part if acc is None else acc + part
        row += a_ref.shape[1]
    o_ref[...] = acc.astype(o_ref.dtype)


def _k_chunk(k, limit=6144):
    best = HEAD_DIM
    for c in range(HEAD_DIM, min(k, limit) + 1, HEAD_DIM):
        if k % c == 0:
            best = c
    return best


def matmul_cat(a_list, w, lead, out_dtype, tm=1024, tn=256):
    m = a_list[0].shape[0]
    k_total, n = w.shape[-2], w.shape[-1]
    assert sum(a.shape[1] for a in a_list) == k_total and m % tm == 0 and n % tn == 0
    chunks = tuple(_k_chunk(a.shape[1]) for a in a_list)
    a_specs = [pl.BlockSpec((tm, a.shape[1]), lambda i, j: (i, 0), pipeline_mode=pl.Buffered(1))
               for a in a_list]
    return pl.pallas_call(
        functools.partial(_mm_cat_kernel, chunks=chunks),
        grid=(m // tm, n // tn),
        in_specs=a_specs + [_wspec(lead, (k_total, tn), lambda i, j: (0, j))],
        out_specs=pl.BlockSpec((tm, tn), lambda i, j: (i, j)),
        out_shape=jax.ShapeDtypeStruct((m, n), out_dtype),
        compiler_params=_params("arbitrary", "arbitrary"),
        name="matmul_cat",
    )(*a_list, w)


def _gateup_kernel(a_ref, w1_ref, w3_ref, o_ref, *, tc):
    a = a_ref[...]
    for n0 in range(0, o_ref.shape[1], tc):
        g = jnp.dot(a, w1_ref[:, n0:n0 + tc].astype(BF16), preferred_element_type=F32)
        u = jnp.dot(a, w3_ref[:, n0:n0 + tc].astype(BF16), preferred_element_type=F32)
        o_ref[:, n0:n0 + tc] = (g * jax.nn.sigmoid(g) * u).astype(o_ref.dtype)


def gate_up(a, w1, w3, lead, tm=1024, tn=512):
    m, k = a.shape
    n = w1.shape[-1]
    tm, tn = min(tm, m), min(tn, n)
    wspec = _wspec(lead, (k, tn), lambda i, j: (0, j))
    return pl.pallas_call(
        functools.partial(_gateup_kernel, tc=min(256, tn)),
        grid=(m // tm, pl.cdiv(n, tn)),
        in_specs=[pl.BlockSpec((tm, k), lambda i, j: (i, 0), pipeline_mode=pl.Buffered(1)), wspec, wspec],
        out_specs=pl.BlockSpec((tm, tn), lambda i, j: (i, j)),
        out_shape=jax.ShapeDtypeStruct((m, n), BF16),
        compiler_params=_params("arbitrary", "arbitrary"),
        name="gate_up",
    )(a, w1, w3)


def _sb_kernel(q_ref, k_ref, v_ref, o_ref, *, t):
    i = pl.program_id(1)
    nch = q_ref.shape[0] // t
    qs = [q_ref[ch * t:(ch + 1) * t, :] for ch in range(nch)]
    row = lax.broadcasted_iota(jnp.int32, (t, t), 0)
    col = lax.broadcasted_iota(jnp.int32, (t, t), 1)
    suffix = (row > col).astype(BF16)
    scale = HEAD_DIM ** -0.5

    def cond(carry):
        step, cs, _ = carry
        top = functools.reduce(jnp.maximum, [jnp.max(c) for c in cs])
        return jnp.logical_and(nch * i + nch - 1 - step >= 0, top >= EXP_UNDERFLOW)

    def body(carry):
        step, cs, accs = carry
        new_c, new_acc = [], []
        for ch in range(nch):
            qblk = nch * i + ch
            j = qblk - step
            jc = jnp.maximum(j, 0)
            start = pl.multiple_of(jc * t, t)
            kj = k_ref[pl.ds(start, t), :]
            vj = v_ref[pl.ds(start, t), :]
            z = _nt(qs[ch], kj) * scale
            mask = (col - row) < jnp.where(j >= 0, (qblk - jc) * t, -2 * t)
            ls_pos = jnp.minimum(z, 0.0) - jnp.log(1.0 + jnp.exp(-jnp.abs(z)))
            lk = jnp.where(mask, ls_pos - z, 0.0)
            later = sum(jnp.dot(p, suffix, preferred_element_type=F32) for p in _split_bf16(lk))
            w = jnp.where(mask, jnp.exp(ls_pos + later + cs[ch]), 0.0)
            new_acc.append(accs[ch] + jnp.dot(w.astype(BF16), vj, preferred_element_type=F32))
            new_c.append(cs[ch] + jnp.sum(lk, axis=1, keepdims=True))
        return step + 1, tuple(new_c), tuple(new_acc)

    init = (jnp.int32(0), tuple(jnp.zeros((t, 1), F32) for _ in range(nch)),
            tuple(jnp.zeros((t, HEAD_DIM), F32) for _ in range(nch)))
    _, _, accs = lax.while_loop(cond, body, init)
    for ch in range(nch):
        o_ref[ch * t:(ch + 1) * t, :] = accs[ch].astype(o_ref.dtype)


def stick_breaking(proj, q_off, k_off, v_off, chains=2):
    s = proj.shape[0]
    t = min(SB_BLOCK, s)
    tq = min(chains * t, s)
    return pl.pallas_call(
        functools.partial(_sb_kernel, t=t),
        grid=(N_HEADS, s // tq),
        in_specs=[pl.BlockSpec((tq, HEAD_DIM), lambda h, i: (i, q_off + h)),
                  pl.BlockSpec((s, HEAD_DIM), lambda h, i: (0, k_off + h)),
                  pl.BlockSpec((s, HEAD_DIM), lambda h, i: (0, v_off + h))],
        out_specs=pl.BlockSpec((tq, HEAD_DIM), lambda h, i: (i, h)),
        out_shape=jax.ShapeDtypeStruct((s, HALF_WIDTH), BF16),
        compiler_params=_params("arbitrary", "arbitrary"),
        name="stick_breaking",
    )(proj, proj, proj)


def _s5_spread(rows, cols, row_period, col_period, row_div, col_div):
    r = lax.broadcasted_iota(jnp.int32, (rows, cols), 0)
    c = lax.broadcasted_iota(jnp.int32, (rows, cols), 1)
    same = jnp.logical_and(r // row_div == c // col_div, r % row_period == c % col_period)
    return same.astype(BF16)


def _s5_own_group(shape, rows_per_group, lanes_per_group):
    r = lax.broadcasted_iota(jnp.int32, shape, 0)
    c = lax.broadcasted_iota(jnp.int32, shape, 1)
    return (r // rows_per_group) % SSM_PACK == (c // lanes_per_group) % SSM_PACK


def _s5_expand(compact, spread, keep):
    full = jnp.dot(compact, spread, preferred_element_type=F32)
    return jnp.where(keep, full, 0.0).astype(BF16)


def _s5_kernel(u_ref, toep_ref, win_ref, vout_ref, ar_ref, ai_ref, y_ref, sin_ref, acc_ref):
    L, H, P = SSM_CHUNK, SSM_GROUP, SSM_STATE
    nc = u_ref.shape[0] // L
    jc = pl.program_id(1)
    per = acc_ref.shape[1] // HEAD_DIM
    xp = [jnp.concatenate([u_ref[pl.ds(2 * p, nc, stride=L), :], u_ref[pl.ds(2 * p + 1, nc, stride=L), :]],
                          axis=1).astype(BF16) for p in range(L // 2)]
    pair = lambda ref, p: ref[0, 0, 2 * p:2 * p + 2].reshape(2 * HEAD_DIM, ref.shape[-1])

    @pl.when(jc == 0)
    def _():
        nstate = sin_ref.shape[1]
        half = nstate // 2
        spread = _s5_spread(2 * P, nstate, P, P, P, half)
        keep = _s5_own_group((2 * HEAD_DIM, nstate), H, P)
        b = sum(jnp.dot(xp[p], _s5_expand(pair(win_ref, p), spread, keep), preferred_element_type=F32)
                for p in range(L // 2))
        er, ei = b[:, :half], b[:, half:]
        row = lax.broadcasted_iota(jnp.int32, er.shape, 0)
        for k in range(int(math.log2(nc))):
            sh = 1 << k
            sr = jnp.where(row >= sh, pltpu.roll(er, sh, axis=0), 0.0)
            si = jnp.where(row >= sh, pltpu.roll(ei, sh, axis=0), 0.0)
            ar, ai = ar_ref[0, k:k + 1, :], ai_ref[0, k:k + 1, :]
            er, ei = er + ar * sr - ai * si, ei + ar * si + ai * sr
        sin_ref[:, :half] = jnp.where(row >= 1, pltpu.roll(er, 1, axis=0), 0.0).astype(BF16)
        sin_ref[:, half:] = jnp.where(row >= 1, pltpu.roll(ei, 1, axis=0), 0.0).astype(BF16)

    spread = _s5_spread(per * H, per * HEAD_DIM, H, H, H, HEAD_DIM)
    keep_state = _s5_own_group((sin_ref.shape[1], per * HEAD_DIM), P, H)
    acc_ref[...] = jnp.dot(sin_ref[...], _s5_expand(vout_ref[0, 0], spread, keep_state),
                           preferred_element_type=F32)
    keep_in = _s5_own_group((2 * HEAD_DIM, per * HEAD_DIM), H, H)
    for p in range(L // 2):
        @pl.when(2 * p < (jc + 1) * per)
        def _(p=p):
            acc_ref[...] += jnp.dot(xp[p], _s5_expand(pair(toep_ref, p), spread, keep_in),
                                    preferred_element_type=F32)
    for t in range(per):
        y_ref[pl.ds(jc * per + t, nc, stride=L), :] = acc_ref[:, t * HEAD_DIM:(t + 1) * HEAD_DIM]


def s5_tables(a_re, a_im, b_re, b_im, c_re, c_im, d_skip, log_dt, ncol):
    L, P, H, G, K8 = SSM_CHUNK, SSM_STATE, SSM_GROUP, SSM_GROUPS, SSM_PACK
    GC = G // K8
    per = L // ncol
    dt = jnp.exp(log_dt.astype(F32))[:, None]
    ar, ai = a_re.astype(F32), a_im.astype(F32)
    mag = jnp.exp(ar * dt)
    lr, li = mag * jnp.cos(ai * dt), mag * jnp.sin(ai * dt)
    den = ar * ar + ai * ai
    nr, ni = lr - 1.0, li
    cr = (nr * ar + ni * ai) / den
    ci = (ni * ar - nr * ai) / den
    br, bi = b_re.astype(F32), b_im.astype(F32)
    bbr = cr[..., None] * br - ci[..., None] * bi
    bbi = cr[..., None] * bi + ci[..., None] * br
    pr, pi = [jnp.ones_like(lr)], [jnp.zeros_like(li)]
    for _ in range(L):
        pr, pi = pr + [pr[-1] * lr - pi[-1] * li], pi + [pr[-1] * li + pi[-1] * lr]
    pr, pi = jnp.stack(pr, 1), jnp.stack(pi, 1)
    ccr, cci = c_re.astype(F32), c_im.astype(F32)
    clr = ccr[:, None] * pr[:, :L, None, :] - cci[:, None] * pi[:, :L, None, :]
    cli = ccr[:, None] * pi[:, :L, None, :] + cci[:, None] * pr[:, :L, None, :]
    kern = (jnp.einsum('gdnp,gph->gdnh', clr, bbr, precision='highest')
            - jnp.einsum('gdnp,gph->gdnh', cli, bbi, precision='highest'))
    kern = kern.at[:, 0].add(jnp.eye(H, dtype=F32)[None] * d_skip.astype(F32).reshape(G, H)[:, :, None])
    tau = jnp.arange(L)
    kflat = kern.transpose(0, 3, 1, 2).astype(BF16).reshape(GC, K8 * H, L * H)
    kflat = jnp.pad(kflat, ((0, 0), (0, 0), ((L - 1) * H, 0)))
    window = lambda j, kap: kflat[:, :, (per * j - kap + L - 1) * H:(per * j - kap + L - 1 + per) * H]
    toep = jnp.stack([jnp.stack([window(j, kap) for kap in range(L)], 1) for j in range(ncol)], 1)
    qr, qi = pr[:, L - 1 - tau], pi[:, L - 1 - tau]
    bt_r, bt_i = bbr.transpose(0, 2, 1)[:, None], bbi.transpose(0, 2, 1)[:, None]
    wre = qr[:, :, None, :] * bt_r - qi[:, :, None, :] * bt_i
    wim = qr[:, :, None, :] * bt_i + qi[:, :, None, :] * bt_r
    win = jnp.concatenate([wre, wim], -1)
    win = win.reshape(GC, K8, L, H, 2 * P).transpose(0, 2, 1, 3, 4).reshape(GC, 1, L, K8 * H, 2 * P)
    pr1, pi1 = pr[:, 1:], pi[:, 1:]
    vre = ccr[:, None] * pr1[:, :, None, :] - cci[:, None] * pi1[:, :, None, :]
    vim = ccr[:, None] * pi1[:, :, None, :] + cci[:, None] * pr1[:, :, None, :]
    vflat = jnp.stack([vre, -vim], 1).transpose(0, 1, 4, 2, 3).astype(BF16)
    vflat = vflat.reshape(GC, K8, 2, P, L * H).transpose(0, 2, 1, 3, 4).reshape(GC, 2 * K8 * P, L * H)
    vout = jnp.stack([vflat[:, :, j * per * H:(j + 1) * per * H] for j in range(ncol)], 1)
    zr, zi = [pr[:, L]], [pi[:, L]]
    for _ in range(15):
        zr, zi = zr + [zr[-1] * zr[-1] - zi[-1] * zi[-1]], zi + [2.0 * zr[-1] * zi[-1]]
    pack = lambda z: jnp.stack(z, 1).reshape(GC, K8, 16, P).transpose(0, 2, 1, 3).reshape(GC, 16, K8 * P)
    return toep.astype(BF16), win.astype(BF16), vout.astype(BF16), pack(zr), pack(zi)


def s5_scan(u, s5_params, ncol=2):
    toep, win, vout, zr, zi = s5_tables(*s5_params, ncol=ncol)
    s, width = u.shape
    L, K8, P = SSM_CHUNK, SSM_PACK, SSM_STATE
    gc = width // HEAD_DIM
    cw = (L // ncol) * HEAD_DIM
    nstate = 2 * K8 * P
    return pl.pallas_call(
        _s5_kernel,
        grid=(gc, ncol),
        in_specs=[pl.BlockSpec((s, HEAD_DIM), lambda g, j: (0, g)),
                  pl.BlockSpec((1, 1) + toep.shape[2:], lambda g, j: (g, j, 0, 0, 0)),
                  pl.BlockSpec((1, 1) + win.shape[2:], lambda g, j: (g, 0, 0, 0, 0)),
                  pl.BlockSpec((1, 1) + vout.shape[2:], lambda g, j: (g, j, 0, 0)),
                  pl.BlockSpec((1, 16, nstate // 2), lambda g, j: (g, 0, 0)),
                  pl.BlockSpec((1, 16, nstate // 2), lambda g, j: (g, 0, 0))],
        out_specs=pl.BlockSpec((s, HEAD_DIM), lambda g, j: (0, g)),
        out_shape=jax.ShapeDtypeStruct((s, width), F32),
        scratch_shapes=[pltpu.VMEM((s // L, nstate), BF16), pltpu.VMEM((s // L, cw), F32)],
        compiler_params=_params("arbitrary", "arbitrary"),
        name="s5_scan",
    )(u, toep, win, vout, zr, zi)


def _glu_kernel(y_ref, yj_ref, w_ref, o_ref):
    yg = jax.nn.gelu(y_ref[...])
    z = jnp.dot(yg.astype(BF16), w_ref[...].astype(BF16), preferred_element_type=F32)
    o_ref[...] = (jax.nn.gelu(yj_ref[...]) * jax.nn.sigmoid(z)).astype(o_ref.dtype)


def gelu_glu(y, glu_w, lead, tm=512, tn=512):
    m, k = y.shape
    tm = min(tm, m)
    return pl.pallas_call(
        _glu_kernel,
        grid=(m // tm, k // tn),
        in_specs=[pl.BlockSpec((tm, k), lambda i, j: (i, 0)),
                  pl.BlockSpec((tm, tn), lambda i, j: (i, j)),
                  _wspec(lead, (k, tn), lambda i, j: (0, j))],
        out_specs=pl.BlockSpec((tm, tn), lambda i, j: (i, j)),
        out_shape=jax.ShapeDtypeStruct((m, k), BF16),
        compiler_params=_params("arbitrary", "arbitrary"),
        name="gelu_glu",
    )(y, y, glu_w)


def _hgrn_head(q, f_logit, v, gate, lb, nw, tril, state):
    tc, d = q.shape
    sub = HGRN_SUB
    nblk = tc // sub
    f = lb + (1.0 - lb) * jax.nn.sigmoid(f_logit)
    k = 1.0 - f
    gcum = sum(jnp.dot(tril, p, preferred_element_type=F32)
               for p in _split_bf16(jnp.log(f), parts=3))
    gtot = gcum[tc - 1:tc, :]
    o = _nt((q * jnp.exp(gcum)).astype(BF16), state.astype(BF16))

    g3, q3, k3, v3 = (x.reshape(nblk, sub, d) for x in (gcum, q, k, v))
    s_idx = lax.broadcasted_iota(jnp.int32, (nblk, sub, d), 1)
    rows = []
    for t in range(sub):
        dec = jnp.exp(jnp.where(s_idx <= t, g3[:, t:t + 1] - g3, -jnp.inf))
        a = jnp.sum(q3[:, t:t + 1] * k3 * dec, axis=2, keepdims=True)
        rows.append(jnp.sum(a * v3, axis=1, keepdims=True))
    o = o + jnp.concatenate(rows, axis=1).reshape(tc, d)

    v16 = v.astype(BF16)
    b = sub
    while 2 * b <= tc:
        m = tc // (2 * b)
        g4, q4, k4 = (x.reshape(m, 2 * b, d) for x in (gcum, q, k))
        ref = g4[:, b:b + 1]
        qd = (q4[:, b:] * jnp.exp(g4[:, b:] - ref)).astype(BF16)
        kd = (k4[:, :b] * jnp.exp(ref - g4[:, :b])).astype(BF16)
        att = jnp.einsum('mqd,mkd->mqk', qd, kd, preferred_element_type=F32)
        ob = jnp.einsum('mqk,mkd->mqd', att.astype(BF16), v16.reshape(m, 2 * b, d)[:, :b],
                        preferred_element_type=F32)
        o = o + jnp.concatenate([jnp.zeros_like(ob), ob], axis=1).reshape(tc, d)
        b *= 2

    kdec = (k * jnp.exp(gtot - gcum)).astype(BF16)
    new_state = state * jnp.exp(gtot) + _tn(v16, kdec)
    on = o * lax.rsqrt(jnp.mean(o * o, axis=-1, keepdims=True) + RMS_EPS) * nw
    return (on * (gate * jax.nn.sigmoid(gate))).astype(BF16), new_state


def _hgrn_kernel(q_ref, f_ref, i_ref, g_ref, lb_ref, nw_ref, o_ref, state_ref):
    @pl.when(pl.program_id(1) == 0)
    def _():
        state_ref[...] = jnp.zeros_like(state_ref)

    tc = q_ref.shape[0]
    d = HEAD_DIM
    r_i = lax.broadcasted_iota(jnp.int32, (tc, tc), 0)
    c_i = lax.broadcasted_iota(jnp.int32, (tc, tc), 1)
    tril = (c_i <= r_i).astype(BF16)
    nw = nw_ref[...]
    for hh in range(q_ref.shape[1] // d):
        sl = slice(hh * d, (hh + 1) * d)
        out, new_state = _hgrn_head(q_ref[:, sl], f_ref[:, sl], i_ref[:, sl], g_ref[:, sl],
                                    lb_ref[:, sl], nw, tril, state_ref[hh])
        o_ref[:, sl] = out
        state_ref[hh] = new_state


def hgrn2(proj, lb, norm_w, offs, tc=512, heads=2):
    s = proj.shape[0]
    tc = min(tc, s)
    w = heads * HEAD_DIM
    assert all(o % heads == 0 for o in offs) and N_HEADS % heads == 0
    specs = [pl.BlockSpec((tc, w), functools.partial(lambda h, i, o: (i, o // heads + h), o=o)) for o in offs]
    return pl.pallas_call(
        _hgrn_kernel,
        grid=(N_HEADS // heads, s // tc),
        in_specs=specs + [pl.BlockSpec((1, w), lambda h, i: (0, h)),
                          pl.BlockSpec((1, HEAD_DIM), lambda h, i: (0, 0))],
        out_specs=pl.BlockSpec((tc, w), lambda h, i: (i, h)),
        out_shape=jax.ShapeDtypeStruct((s, HALF_WIDTH), BF16),
        scratch_shapes=[pltpu.VMEM((heads, HEAD_DIM, HEAD_DIM), F32)],
        compiler_params=_params("arbitrary", "arbitrary"),
        name="hgrn2",
    )(proj, proj, proj, proj, lb.reshape(1, HALF_WIDTH).astype(F32), norm_w.reshape(1, HEAD_DIM).astype(F32))


def rope_tables(s):
    half = ROPE_DIM // 2
    inv = ROPE_THETA ** (-jnp.arange(half, dtype=F32) / half)
    ang = jnp.arange(s).astype(F32)[:, None] * inv[None, :]
    cos, sin = jnp.cos(ang), jnp.sin(ang)
    one = jnp.ones((s, HEAD_DIM - ROPE_DIM), F32)
    zero = jnp.zeros((s, HEAD_DIM - ROPE_DIM), F32)
    z16 = jnp.zeros((s, half), F32)
    c = jnp.concatenate([cos, cos, one], 1)
    s_up = jnp.concatenate([z16, sin, zero], 1)
    s_dn = jnp.concatenate([-sin, z16, zero], 1)
    return c, s_up, s_dn


def _rope(x, c, s_up, s_dn):
    half = ROPE_DIM // 2
    return (x * c + pltpu.roll(x, half, axis=1) * s_up
            + pltpu.roll(x, HEAD_DIM - half, axis=1) * s_dn)


def _moba_prep_kernel(k_ref, v_ref, c_ref, su_ref, sd_ref, kr_ref, vt_ref, km_ref):
    c, su, sd = c_ref[...], su_ref[...], sd_ref[...]
    for h in range(N_HEADS):
        sl = slice(h * HEAD_DIM, (h + 1) * HEAD_DIM)
        kr = _rope(k_ref[:, sl], c, su, sd)
        kr_ref[:, sl] = kr.astype(BF16)
        km_ref[0, :, sl] = jnp.mean(kr, axis=0, keepdims=True)
    vt_ref[0] = v_ref[...].T.astype(BF16)


def moba_prep(proj, k_blk, v_blk, tables):
    s = proj.shape[0]
    nb = s // MOBA_BLOCK
    tab = pl.BlockSpec((MOBA_BLOCK, HEAD_DIM), lambda n: (n, 0))
    kr, vt, km = pl.pallas_call(
        _moba_prep_kernel,
        grid=(nb,),
        in_specs=[pl.BlockSpec((MOBA_BLOCK, HALF_WIDTH), lambda n: (n, k_blk)),
                  pl.BlockSpec((MOBA_BLOCK, HALF_WIDTH), lambda n: (n, v_blk)),
                  tab, tab, tab],
        out_specs=[pl.BlockSpec((MOBA_BLOCK, HALF_WIDTH), lambda n: (n, 0)),
                   pl.BlockSpec((1, HALF_WIDTH, MOBA_BLOCK), lambda n: (n, 0, 0)),
                   pl.BlockSpec((1, 1, HALF_WIDTH), lambda n: (n, 0, 0))],
        out_shape=[jax.ShapeDtypeStruct((s, HALF_WIDTH), BF16),
                   jax.ShapeDtypeStruct((nb, HALF_WIDTH, MOBA_BLOCK), BF16),
                   jax.ShapeDtypeStruct((nb, 1, HALF_WIDTH), F32)],
        compiler_params=_params("arbitrary"),
        name="moba_prep",
    )(proj, proj, *tables)
    return kr, vt, km.reshape(nb, HALF_WIDTH)


def _moba_kernel(q_ref, c_ref, su_ref, sd_ref, k_ref, vt_ref, km_ref, o_ref, sel_ref, s_ref, *, unroll):
    own = pl.program_id(1)
    blk, d = MOBA_BLOCK, HEAD_DIM
    nb = km_ref.shape[0]
    heads = q_ref.shape[1] // d
    cols = lambda hh: slice(hh * d, (hh + 1) * d)
    c, su, sd = c_ref[...], su_ref[...], sd_ref[...]
    o0 = pl.multiple_of(own * blk, blk)

    qs, init = [], []
    for hh in range(heads):
        qf = _rope(q_ref[:, cols(hh)], c, su, sd)
        q = (qf * d ** -0.5).astype(BF16)
        qs.append(q)
        gate = _nt(km_ref[:, cols(hh)], qf, precision=lax.Precision.HIGHEST)
        row = lax.broadcasted_iota(jnp.int32, gate.shape, 0)
        g = jnp.where(row < own, gate, NEG)
        sel = jnp.zeros(gate.shape, F32)
        for j in range(MOBA_TOPK):
            m = jnp.max(g, axis=0, keepdims=True)
            idx = jnp.min(jnp.where(g == m, row, nb), axis=0, keepdims=True)
            pick = row == idx
            sel = jnp.where(pick, jnp.where(j < own, 1.0, 0.0), sel)
            g = jnp.where(pick, -jnp.inf, g)
        sel_ref[hh] = sel
        s0 = _nt(k_ref[pl.ds(o0, blk), cols(hh)], q)
        kpos = lax.broadcasted_iota(jnp.int32, s0.shape, 0)
        qpos = lax.broadcasted_iota(jnp.int32, s0.shape, 1)
        s0 = jnp.where(kpos <= qpos, s0, NEG)
        m0 = jnp.max(s0, axis=0, keepdims=True)
        p0 = jnp.exp(s0 - m0)
        l0 = jnp.sum(p0, axis=0, keepdims=True)
        acc0 = jnp.dot(vt_ref[own, cols(hh), :], p0.astype(BF16), preferred_element_type=F32)
        init.append((m0, l0, acc0))

    def raw_scores(it):
        for hh in range(heads):
            for u in range(unroll):
                n = jnp.minimum(it * unroll + u, nb - 1)
                start = pl.multiple_of(n * blk, blk)
                s_ref[it % 2, hh, u] = _nt(k_ref[pl.ds(start, blk), cols(hh)], qs[hh])

    def body(it, carry):
        out = []
        for hh in range(heads):
            m, l, acc = carry[hh]
            scores = [jnp.where(sel_ref[hh, pl.ds(it * unroll + u, 1), :] > 0.0,
                                s_ref[it % 2, hh, u], NEG) for u in range(unroll)]
            m_new = functools.reduce(jnp.maximum, [jnp.max(sn, axis=0, keepdims=True) for sn in scores], m)
            alpha = jnp.exp(m - m_new)
            l, acc = alpha * l, alpha * acc
            for u, sn in enumerate(scores):
                p = jnp.exp(sn - m_new)
                l = l + jnp.sum(p, axis=0, keepdims=True)
                acc = acc + jnp.dot(vt_ref[it * unroll + u, cols(hh), :], p.astype(BF16),
                                    preferred_element_type=F32)
            out.append((m_new, l, acc))
        raw_scores(it + 1)
        return tuple(out)

    trips = (own + unroll - 1) // unroll
    raw_scores(0)
    final = lax.fori_loop(0, trips, body, tuple(init))
    for hh in range(heads):
        _, l, acc = final[hh]
        o_ref[:, cols(hh)] = (acc / l).T.astype(o_ref.dtype)


def moba(proj, q_off, kr, vt, kmean, tables, unroll=4, heads=4):
    s = proj.shape[0]
    nb = s // MOBA_BLOCK
    assert nb % unroll == 0 and q_off % heads == 0 and N_HEADS % heads == 0
    w = heads * HEAD_DIM
    tab = pl.BlockSpec((MOBA_BLOCK, HEAD_DIM), lambda h, i: (i, 0))
    return pl.pallas_call(
        functools.partial(_moba_kernel, unroll=unroll),
        grid=(N_HEADS // heads, nb),
        in_specs=[pl.BlockSpec((MOBA_BLOCK, w), lambda h, i: (i, q_off // heads + h)),
                  tab, tab, tab,
                  pl.BlockSpec((s, w), lambda h, i: (0, h)),
                  pl.BlockSpec((nb, w, MOBA_BLOCK), lambda h, i: (0, h, 0)),
                  pl.BlockSpec((nb, w), lambda h, i: (0, h))],
        out_specs=pl.BlockSpec((MOBA_BLOCK, w), lambda h, i: (i, h)),
        out_shape=jax.ShapeDtypeStruct((s, HALF_WIDTH), BF16),
        scratch_shapes=[pltpu.VMEM((heads, nb, MOBA_BLOCK), F32),
                        pltpu.VMEM((2, heads, unroll, MOBA_BLOCK, MOBA_BLOCK), F32)],
        compiler_params=_params("arbitrary", "arbitrary"),
        name="moba",
    )(proj, *tables, kr, vt, kmean)


def _even_mixer(h, w_in, w_out, e, s5_params, glu_w):
    qkv = matmul(h, w_in, (e,), BF16, col0=0, ncols=3 * HALF_WIDTH)
    u = matmul(h, w_in, (e,), F32, col0=3 * HALF_WIDTH, ncols=HALF_WIDTH)
    o_a = stick_breaking(qkv, 0, N_HEADS, 2 * N_HEADS)
    o_b = gelu_glu(s5_scan(u, s5_params), glu_w, (e,))
    return matmul_cat([o_a, o_b], w_out, (e,), BF16, tn=512)


def _odd_mixer(h, w_in, w_out, o, lb, norm_w):
    s = h.shape[0]
    proj = matmul(h, w_in, (o,), F32)
    o_c = hgrn2(proj, lb, norm_w, (0, N_HEADS, 2 * N_HEADS, 3 * N_HEADS))
    tables = rope_tables(s)
    kr, vt, kmean = moba_prep(proj, 5, 6, tables)
    o_d = moba(proj, 4 * N_HEADS, kr, vt, kmean, tables)
    return matmul_cat([o_c, o_d], w_out, (o,), BF16, tn=512)


def kernel(x, c, ada_w, ada_table, norm_pre, norm_post, ffn_w1, ffn_w3, ffn_w2, ev_w_in, ev_w_out,
           s5_a_re, s5_a_im, s5_b_re, s5_b_im, s5_c_re, s5_c_im, s5_d, s5_log_dt, s5_glu_w,
           od_w_in, od_w_out, hgrn_lb, hgrn_norm_w):
    bsz, seq, d = x.shape
    depth = ada_table.shape[0]
    mod_shared = ada_project(c, ada_w).reshape(bsz, 9, d)
    lb_cum = jnp.cumsum(jax.nn.softmax(hgrn_lb.astype(F32), axis=0), axis=0)
    lb_all = lb_cum - lb_cum[:1]

    outs = []
    for b in range(bsz):
        xb = x[b]
        mods = [mod_shared[b] + ada_table[layer] for layer in range(depth)]
        subs = [(layer, slot) for layer in range(depth) for slot in range(3)]
        first = subs[0]
        h = pre_norm(xb, norm_pre[first[0], first[1]], mods[first[0]][1], mods[first[0]][0])
        for n, (layer, slot) in enumerate(subs):
            mod = mods[layer]
            if slot == 1:
                if layer % 2 == 0:
                    e = layer // 2
                    y = _even_mixer(h, ev_w_in, ev_w_out, e,
                                    (s5_a_re[e], s5_a_im[e], s5_b_re[e], s5_b_im[e], s5_c_re[e], s5_c_im[e],
                                     s5_d[e], s5_log_dt[e]), s5_glu_w)
                else:
                    o = layer // 2
                    y = _odd_mixer(h, od_w_in, od_w_out, o, lb_all[layer], hgrn_norm_w[o])
                res_w = 1.0
            else:
                f = slot // 2
                g = gate_up(h, ffn_w1, ffn_w3, (layer, f))
                y = matmul_cat([g], ffn_w2, (layer, f), BF16)
                res_w = 0.5
            nxt = None
            if n + 1 < len(subs):
                nl, ns = subs[n + 1]
                nxt = (norm_pre[nl, ns], mods[nl][3 * ns + 1], mods[nl][3 * ns])
            xb, h = post_norm(y, xb, norm_post[layer, slot], mod[3 * slot + 2], res_w, nxt)
        outs.append(xb)
    return jnp.stack(outs, axis=0)
```
